```python
import math
import jax, jax.numpy as jnp
from jax import lax
import numpy as np

D_MODEL = 1024
BATCH = 32
SEQ = 2048
DEPTH = 1
DEC_BATCH = 16
DEC_SEQ = 32
PAST_LEN = 1024

CHUNK = 64
N_DIFF_HEADS = 4
DIFF_HEAD_DIM = 64
DIFF_WIDTH = N_DIFF_HEADS * 2 * DIFF_HEAD_DIM
CONV_CH = D_MODEL - DIFF_WIDTH
CONV_WIDTH = 31
MIX_WIDTH = DIFF_WIDTH + CONV_CH
IN_COLS = 3 * DIFF_WIDTH + 2 * CONV_CH
N_MEM = 256
N_CROSS_HEADS = 4
CROSS_HEAD_DIM = D_MODEL // N_CROSS_HEADS
N_EXPERTS = 32
TOP_K = 4
D_FF = D_MODEL
SWIGLU_LIMIT = 7.0
SWIGLU_ALPHA = 1.702
Q_BLOCK = 128
MOE_BLOCK = 512
EPS = 1e-6
NEG_INF = -1e30

kernel_name = "hymba_diffattn_conformer_moe_stream_step"


def rms_norm(x, g):
    xf = x.astype(jnp.float32)
    y = xf * lax.rsqrt(jnp.mean(xf * xf, axis=-1, keepdims=True) + EPS)
    return (y * g.astype(jnp.float32)).astype(x.dtype)


def lambda_init_of(layer):
    return 0.8 - 0.6 * math.exp(-0.3 * layer)


def input_projection(h, w_in):
    B, T, _ = h.shape
    z = h @ w_in
    q = z[..., :DIFF_WIDTH].reshape(B, T, N_DIFF_HEADS, 2 * DIFF_HEAD_DIM)
    k = z[..., DIFF_WIDTH:2 * DIFF_WIDTH].reshape(B, T, N_DIFF_HEADS, 2 * DIFF_HEAD_DIM)
    v = z[..., 2 * DIFF_WIDTH:3 * DIFF_WIDTH].reshape(B, T, N_DIFF_HEADS, 2 * DIFF_HEAD_DIM)
    g = z[..., 3 * DIFF_WIDTH:]
    u = g[..., :CONV_CH] * jax.nn.sigmoid(g[..., CONV_CH:])
    return q, k, v, u


def diff_lambda(lq1, lk1, lq2, lk2, lam_init):
    f32 = jnp.float32
    return (jnp.exp(jnp.sum(lq1.astype(f32) * lk1.astype(f32)))
            - jnp.exp(jnp.sum(lq2.astype(f32) * lk2.astype(f32))) + lam_init)


def diff_attention(q, k, v, q_pos, k_pos, lam, subln_g, lam_init):
    B, Tq = q.shape[:2]
    Tk = k.shape[1]
    qc = q.reshape(B, Tq, N_DIFF_HEADS, 2, DIFF_HEAD_DIM)
    kc = k.reshape(B, Tk, N_DIFF_HEADS, 2, DIFF_HEAD_DIM)
    s = jnp.einsum('bqhcd,bkhcd->bchqk', qc, kc).astype(jnp.float32) * (DIFF_HEAD_DIM ** -0.5)
    visible = (k_pos[None, :] // CHUNK) <= (q_pos[:, None] // CHUNK)
    s = jnp.where(visible, s, NEG_INF)
    p = jax.nn.softmax(s, axis=-1)
    a = p[:, 0] - lam * p[:, 1]
    o = jnp.einsum('bhqk,bkhd->bqhd', a, v.astype(jnp.float32))
    o = o * lax.rsqrt(jnp.mean(o * o, axis=-1, keepdims=True) + EPS)
    o = o * subln_g.astype(jnp.float32) * (1.0 - lam_init)
    return o.reshape(B, Tq, DIFF_WIDTH).astype(q.dtype)


def prompt_diff_attention(q, k, v, lam, subln_g, lam_init):
    B, S = q.shape[:2]
    nb = S // Q_BLOCK
    qb = q.reshape(B, nb, Q_BLOCK, N_DIFF_HEADS, 2 * DIFF_HEAD_DIM).transpose(1, 0, 2, 3, 4)
    pos = jnp.arange(S, dtype=jnp.int32)
    pb = pos.reshape(nb, Q_BLOCK)
    ob = lax.map(lambda a: diff_attention(a[0], k, v, a[1], pos, lam, subln_g, lam_init), (qb, pb))
    return ob.transpose(1, 0, 2, 3).reshape(B, S, DIFF_WIDTH)


def conv_module_tail(u_ext, conv_w, conv_b, ln_g, ln_b):
    c = lax.conv_general_dilated(u_ext, conv_w[:, None, :], (1,), 'VALID',
                                 dimension_numbers=('NWC', 'WIO', 'NWC'),
                                 feature_group_count=CONV_CH) + conv_b
    cf = c.astype(jnp.float32)
    mu = jnp.mean(cf, axis=-1, keepdims=True)
    var = jnp.mean(jnp.square(cf - mu), axis=-1, keepdims=True)
    cn = (cf - mu) * lax.rsqrt(var + EPS) * ln_g.astype(jnp.float32) + ln_b.astype(jnp.float32)
    return jax.nn.silu(cn).astype(u_ext.dtype)


def memory_kv(mem, norm_mem, w_ck, w_cv):
    B, M, _ = mem.shape
    mn = rms_norm(mem, norm_mem)
    mk = (mn @ w_ck).reshape(B, M, N_CROSS_HEADS, CROSS_HEAD_DIM)
    mv = (mn @ w_cv).reshape(B, M, N_CROSS_HEADS, CROSS_HEAD_DIM)
    return mk, mv


def cross_attention(h, mk, mv, w_cq, w_co):
    B, T, _ = h.shape
    q = (h @ w_cq).reshape(B, T, N_CROSS_HEADS, CROSS_HEAD_DIM)
    s = jnp.einsum('bqhd,bmhd->bhqm', q, mk).astype(jnp.float32) * (CROSS_HEAD_DIM ** -0.5)
    p = jax.nn.softmax(s, axis=-1)
    o = jnp.einsum('bhqm,bmhd->bqhd', p, mv.astype(jnp.float32)).astype(h.dtype)
    return o.reshape(B, T, D_MODEL) @ w_co


def moe_ffn(h, w_router, b_router, w_gu, b_gu, w_down, b_down):
    B, T, D = h.shape
    n = B * T
    t = h.reshape(n, D)
    logits = (t @ w_router + b_router).astype(jnp.float32)
    top_v, top_i = lax.top_k(logits, TOP_K)
    gate_k = jax.nn.softmax(top_v, axis=-1)
    gates = jnp.einsum('tk,tke->te', gate_k, jax.nn.one_hot(top_i, N_EXPERTS, dtype=jnp.float32))
    n_pad = (-n) % MOE_BLOCK
    t = jnp.pad(t, ((0, n_pad), (0, 0)))
    gates = jnp.pad(gates, ((0, n_pad), (0, 0)))
    nb = (n + n_pad) // MOE_BLOCK

    def block(args):
        tb, gb = args
        gu = jnp.einsum('td,edf->tef', tb, w_gu) + b_gu
        gate = jnp.minimum(gu[..., 0::2], SWIGLU_LIMIT)
        up = jnp.clip(gu[..., 1::2], -SWIGLU_LIMIT, SWIGLU_LIMIT)
        act = (up + 1.0) * gate * jax.nn.sigmoid(SWIGLU_ALPHA * gate)
        gbc = gb.astype(act.dtype)
        act = act * gbc[..., None]
        return jnp.einsum('tef,efd->td', act, w_down) + gbc @ b_down

    out = lax.map(block, (t.reshape(nb, MOE_BLOCK, D), gates.reshape(nb, MOE_BLOCK, N_EXPERTS)))
    return out.reshape(nb * MOE_BLOCK, D)[:n].reshape(B, T, D)


def setup_inputs(seed: int = 0) -> dict:
    key = jax.random.key(seed)
    ks = jax.random.split(key, 40)
    f32 = jnp.float32
    nrm = lambda i, shape, scale=1.0: (jax.random.normal(ks[i], shape, f32) * scale)
    gain = lambda i, shape: 1.0 + 0.01 * jax.random.normal(ks[i], shape, f32)
    L = DEPTH
    kv_shape = (L, DEC_BATCH, PAST_LEN, N_DIFF_HEADS, 2 * DIFF_HEAD_DIM)
    mem_shape = (L, DEC_BATCH, N_MEM, N_CROSS_HEADS, CROSS_HEAD_DIM)
    return {
        "x_prompt": nrm(0, (BATCH, SEQ, D_MODEL)),
        "x_sample": nrm(1, (DEC_BATCH, DEC_SEQ, D_MODEL)),
        "cache_k": nrm(2, kv_shape),
        "cache_v": nrm(3, kv_shape),
        "state_conv": nrm(4, (L, DEC_BATCH, CONV_WIDTH - 1, CONV_CH), 0.5),
        "cache_mem_k": nrm(5, mem_shape),
        "cache_mem_v": nrm(6, mem_shape),
        "mem_prompt": nrm(7, (BATCH, N_MEM, D_MODEL)),
        "norm_mix": gain(8, (L, D_MODEL)),
        "w_in": nrm(9, (L, D_MODEL, IN_COLS), D_MODEL ** -0.5),
        "lambda_q1": nrm(10, (L, DIFF_HEAD_DIM), 0.1),
        "lambda_k1": nrm(11, (L, DIFF_HEAD_DIM), 0.1),
        "lambda_q2": nrm(12, (L, DIFF_HEAD_DIM), 0.1),
        "lambda_k2": nrm(13, (L, DIFF_HEAD_DIM), 0.1),
        "subln_g": gain(14, (L, 2 * DIFF_HEAD_DIM)),
        "conv_w": nrm(15, (L, CONV_WIDTH, CONV_CH), CONV_WIDTH ** -0.5),
        "conv_b": nrm(16, (L, CONV_CH), 0.01),
        "conv_ln_g": gain(17, (L, CONV_CH)),
        "conv_ln_b": nrm(18, (L, CONV_CH), 0.01),
        "w_out": nrm(19, (L, MIX_WIDTH, D_MODEL), MIX_WIDTH ** -0.5),
        "norm_cross": gain(20, (L, D_MODEL)),
        "norm_mem": gain(21, (L, D_MODEL)),
        "w_cq": nrm(22, (L, D_MODEL, D_MODEL), D_MODEL ** -0.5),
        "w_ck": nrm(23, (L, D_MODEL, D_MODEL), D_MODEL ** -0.5),
        "w_cv": nrm(24, (L, D_MODEL, D_MODEL), D_MODEL ** -0.5),
        "w_co": nrm(25, (L, D_MODEL, D_MODEL), D_MODEL ** -0.5),
        "norm_moe": gain(26, (L, D_MODEL)),
        "w_router": nrm(27, (L, D_MODEL, N_EXPERTS), D_MODEL ** -0.5),
        "b_router": nrm(28, (L, N_EXPERTS), 0.01),
        "w_gu": nrm(29, (L, N_EXPERTS, D_MODEL, 2 * D_FF), D_MODEL ** -0.5),
        "b_gu": nrm(30, (L, N_EXPERTS, 2 * D_FF), 0.01),
        "w_down": nrm(31, (L, N_EXPERTS, D_FF, D_MODEL), D_FF ** -0.5),
        "b_down": nrm(32, (L, N_EXPERTS, D_MODEL), 0.01),
        "norm_final": gain(33, (D_MODEL,)),
    }


def reference(x_prompt, x_sample, cache_k, cache_v, state_conv, cache_mem_k, cache_mem_v, mem_prompt,
              norm_mix, w_in, lambda_q1, lambda_k1, lambda_q2, lambda_k2, subln_g,
              conv_w, conv_b, conv_ln_g, conv_ln_b, w_out,
              norm_cross, norm_mem, w_cq, w_ck, w_cv, w_co,
              norm_moe, w_router, b_router, w_gu, b_gu, w_down, b_down, norm_final):
    x = x_prompt
    k_p, v_p, cs_p, mk_p, mv_p = [], [], [], [], []
    for l in range(DEPTH):
        lam_init = lambda_init_of(l)
        lam = diff_lambda(lambda_q1[l], lambda_k1[l], lambda_q2[l], lambda_k2[l], lam_init)
        h = rms_norm(x, norm_mix[l])
        q, k, v, u = input_projection(h, w_in[l])
        ao = prompt_diff_attention(q, k, v, lam, subln_g[l], lam_init)
        u_ext = jnp.pad(u, ((0, 0), (CONV_WIDTH - 1, 0), (0, 0)))
        co = conv_module_tail(u_ext, conv_w[l], conv_b[l], conv_ln_g[l], conv_ln_b[l])
        x = x + jnp.concatenate([ao, co], axis=-1) @ w_out[l]
        mk, mv = memory_kv(mem_prompt, norm_mem[l], w_ck[l], w_cv[l])
        x = x + cross_attention(rms_norm(x, norm_cross[l]), mk, mv, w_cq[l], w_co[l])
        x = x + moe_ffn(rms_norm(x, norm_moe[l]), w_router[l], b_router[l], w_gu[l], b_gu[l],
                        w_down[l], b_down[l])
        k_p.append(k)
        v_p.append(v)
        cs_p.append(u[:, -(CONV_WIDTH - 1):])
        mk_p.append(mk)
        mv_p.append(mv)
    y_prompt = rms_norm(x, norm_final)

    x = x_sample
    T = x.shape[1]
    past = cache_k.shape[2]
    q_pos = past + jnp.arange(T, dtype=jnp.int32)
    k_pos = jnp.arange(past + T, dtype=jnp.int32)
    k_s, v_s, cs_s = [], [], []
    for l in range(DEPTH):
        lam_init = lambda_init_of(l)
        lam = diff_lambda(lambda_q1[l], lambda_k1[l], lambda_q2[l], lambda_k2[l], lam_init)
        h = rms_norm(x, norm_mix[l])
        q, k, v, u = input_projection(h, w_in[l])
        k_all = jnp.concatenate([cache_k[l], k], axis=1)
        v_all = jnp.concatenate([cache_v[l], v], axis=1)
        ao = diff_attention(q, k_all, v_all, q_pos, k_pos, lam, subln_g[l], lam_init)
        u_ext = jnp.concatenate([state_conv[l], u], axis=1)
        co = conv_module_tail(u_ext, conv_w[l], conv_b[l], conv_ln_g[l], conv_ln_b[l])
        x = x + jnp.concatenate([ao, co], axis=-1) @ w_out[l]
        x = x + cross_attention(rms_norm(x, norm_cross[l]), cache_mem_k[l], cache_mem_v[l], w_cq[l], w_co[l])
        x = x + moe_ffn(rms_norm(x, norm_moe[l]), w_router[l], b_router[l], w_gu[l], b_gu[l],
                        w_down[l], b_down[l])
        k_s.append(k)
        v_s.append(v)
        cs_s.append(u_ext[:, -(CONV_WIDTH - 1):])
    y_sample = rms_norm(x, norm_final)

    new_k_prompt = jnp.stack(k_p)
    new_v_prompt = jnp.stack(v_p)
    conv_state_prompt = jnp.stack(cs_p)
    mem_k_prompt = jnp.stack(mk_p)
    mem_v_prompt = jnp.stack(mv_p)
    new_k_sample = jnp.stack(k_s)
    new_v_sample = jnp.stack(v_s)
    conv_state_sample = jnp.stack(cs_s)
    return (y_prompt, y_sample, new_k_prompt, new_v_prompt, conv_state_prompt, mem_k_prompt, mem_v_prompt,
            new_k_sample, new_v_sample, conv_state_sample)
```

```python
import functools
import math

import jax
import jax.numpy as jnp
from jax import lax
from jax.experimental import pallas as pl
from jax.experimental.pallas import tpu as pltpu

F32 = jnp.float32
BF16 = jnp.bfloat16
I32 = jnp.int32

EPS = 1e-6
CHUNK = 64
N_DIFF_HEADS = 4
DIFF_HEAD_DIM = 64
DIFF_WIDTH = N_DIFF_HEADS * 2 * DIFF_HEAD_DIM
CONV_WIDTH = 31
N_CROSS_HEADS = 4
TOP_K = 4
SWIGLU_LIMIT = 7.0
SWIGLU_ALPHA = 1.702
NEG_INF = -1e30

V7X_LANES = 128
V7X_BF16_ROWS_PER_TILE = 16
V7X_VMEM_LIMIT_BYTES = 56 * 1024 * 1024

ROUTE_BLOCK = 256
PIECE_ROWS = 64
GROUP = V7X_BF16_ROWS_PER_TILE
FFN_TILE = 512
NO_SLOT = -(1 << 20)

_NT = (((1,), (1,)), ((), ()))
_TN = (((0,), (0,)), ((), ()))


def _cparams(*sem):
    return pltpu.CompilerParams(dimension_semantics=sem, vmem_limit_bytes=V7X_VMEM_LIMIT_BYTES)


def _sds(shape, dtype):
    return jax.ShapeDtypeStruct(shape, dtype)


def _rms(x, g):
    return x * lax.rsqrt(jnp.mean(x * x, axis=-1, keepdims=True) + EPS) * g


def _softmax_parts(pieces):
    m = functools.reduce(jnp.maximum, [jnp.max(s, axis=-1, keepdims=True) for s in pieces])
    es = [jnp.exp(s - m) for s in pieces]
    tot = functools.reduce(jnp.add, [jnp.sum(e, axis=-1, keepdims=True) for e in es])
    return es, 1.0 / tot


def _in_proj_body(x_ref, g_ref, w_ref, k_ref, v_ref, u_ref, qb_ref, kb_ref, vb_ref, *, cc):
    h = _rms(x_ref[...], g_ref[...]).astype(BF16)

    def proj(lo, n):
        return jnp.dot(h, w_ref[:, lo:lo + n], preferred_element_type=F32)

    dw = DIFF_WIDTH
    qb_ref[...] = (proj(0, dw) * (DIFF_HEAD_DIM ** -0.5)).astype(BF16)
    k = proj(dw, dw)
    k_ref[...] = k
    kb_ref[...] = k.astype(BF16)
    v = proj(2 * dw, dw)
    v_ref[...] = v
    vb_ref[...] = v.astype(BF16)
    u_ref[...] = proj(3 * dw, cc) * jax.nn.sigmoid(proj(3 * dw + cc, cc))


def _in_proj(x, g, w, tm):
    n, d = x.shape
    cols = w.shape[1]
    cc = (cols - 3 * DIFF_WIDTH) // 2
    row = lambda i: (i, 0)
    fix = lambda i: (0, 0)
    return pl.pallas_call(
        functools.partial(_in_proj_body, cc=cc),
        grid=(n // tm,),
        in_specs=[pl.BlockSpec((tm, d), row), pl.BlockSpec((1, d), fix), pl.BlockSpec((d, cols), fix)],
        out_specs=[pl.BlockSpec((tm, DIFF_WIDTH), row), pl.BlockSpec((tm, DIFF_WIDTH), row), pl.BlockSpec((tm, cc), row),
                   pl.BlockSpec((tm, DIFF_WIDTH), row), pl.BlockSpec((tm, DIFF_WIDTH), row),
                   pl.BlockSpec((tm, DIFF_WIDTH), row)],
        out_shape=[_sds((n, DIFF_WIDTH), F32), _sds((n, DIFF_WIDTH), F32), _sds((n, cc), F32),
                   _sds((n, DIFF_WIDTH), BF16), _sds((n, DIFF_WIDTH), BF16), _sds((n, DIFF_WIDTH), BF16)],
        compiler_params=_cparams("arbitrary"),
        name="in_proj",
    )(x, g, w)


def _diff_lambda(lq1, lk1, lq2, lk2, lam_init):
    return (jnp.exp(jnp.sum(lq1[...] * lk1[...], axis=-1, keepdims=True))
            - jnp.exp(jnp.sum(lq2[...] * lk2[...], axis=-1, keepdims=True)) + lam_init)


def _split_components(q):
    lane = lax.broadcasted_iota(I32, q.shape, 1)
    zero = jnp.zeros_like(q)
    return jnp.where(lane < DIFF_HEAD_DIM, q, zero), jnp.where(lane >= DIFF_HEAD_DIM, q, zero)


def _sub_ln(o, sg, lam_init):
    o = o * lax.rsqrt(jnp.mean(o * o, axis=-1, keepdims=True) + EPS)
    return o * sg * (1.0 - lam_init)


def _diff_attn_prompt_body(lq1, lk1, lq2, lk2, sg_ref, q_ref, k_ref, v_ref, o_ref, *, tq, lam_init):
    lam = _diff_lambda(lq1, lk1, lq2, lk2, lam_init)
    seq = q_ref.shape[0]
    vis = (lax.broadcasted_iota(I32, (tq, tq), 1) // CHUNK) <= (lax.broadcasted_iota(I32, (tq, tq), 0) // CHUNK)
    for i in range(seq // tq):
        lo, ext = i * tq, (i + 1) * tq
        es, rs = [], []
        for qc in _split_components(q_ref[lo:ext, :]):
            s_diag = lax.dot_general(qc, k_ref[lo:ext, :], _NT, preferred_element_type=F32)
            pieces = [jnp.where(vis, s_diag, NEG_INF)]
            if i > 0:
                pieces.insert(0, lax.dot_general(qc, k_ref[0:lo, :], _NT, preferred_element_type=F32))
            e, r = _softmax_parts(pieces)
            es.append(e[0] if len(e) == 1 else jnp.concatenate(e, axis=1))
            rs.append(r)
        a = es[0] * rs[0] - es[1] * (lam * rs[1])
        o = jnp.dot(a.astype(BF16), v_ref[0:ext, :], preferred_element_type=F32)
        o_ref[lo:ext, :] = _sub_ln(o, sg_ref[...], lam_init).astype(o_ref.dtype)


def _diff_attn_prompt(lams, sg, qb, kb, vb, batch, seq, lam_init):
    hw = 2 * DIFF_HEAD_DIM
    tq = min(256, seq)
    blk = pl.BlockSpec((seq, hw), lambda b, h: (b, h))
    small = pl.BlockSpec((1, DIFF_HEAD_DIM), lambda b, h: (0, 0))
    return pl.pallas_call(
        functools.partial(_diff_attn_prompt_body, tq=tq, lam_init=lam_init),
        grid=(batch, N_DIFF_HEADS),
        in_specs=[small, small, small, small, pl.BlockSpec((1, hw), lambda b, h: (0, 0)), blk, blk, blk],
        out_specs=blk,
        out_shape=_sds((batch * seq, DIFF_WIDTH), BF16),
        compiler_params=_cparams("arbitrary", "arbitrary"),
        name="diff_attn_prompt",
    )(*lams, sg, qb, kb, vb)


def _diff_attn_sample_body(lq1, lk1, lq2, lk2, sg_ref, q_ref, k_ref, v_ref, ck_ref, cv_ref, o_ref, *, lam_init):
    lam = _diff_lambda(lq1, lk1, lq2, lk2, lam_init)
    t = q_ref.shape[0]
    past = ck_ref.shape[0]
    ck = ck_ref[...].astype(BF16)
    q_chunk = (past + lax.broadcasted_iota(I32, (t, 1), 0)) // CHUNK
    vis_past = (lax.broadcasted_iota(I32, (t, past), 1) // CHUNK) <= q_chunk
    vis_new = ((past + lax.broadcasted_iota(I32, (t, t), 1)) // CHUNK) <= q_chunk
    es, rs = [], []
    for qc in _split_components(q_ref[...]):
        s_past = lax.dot_general(qc, ck, _NT, preferred_element_type=F32)
        s_new = lax.dot_general(qc, k_ref[...], _NT, preferred_element_type=F32)
        e, r = _softmax_parts([jnp.where(vis_past, s_past, NEG_INF), jnp.where(vis_new, s_new, NEG_INF)])
        es.append(e)
        rs.append(r)
    a_past = es[0][0] * rs[0] - es[1][0] * (lam * rs[1])
    a_new = es[0][1] * rs[0] - es[1][1] * (lam * rs[1])
    o = (jnp.dot(a_past.astype(BF16), cv_ref[...].astype(BF16), preferred_element_type=F32)
         + jnp.dot(a_new.astype(BF16), v_ref[...], preferred_element_type=F32))
    o_ref[...] = _sub_ln(o, sg_ref[...], lam_init).astype(o_ref.dtype)


def _diff_attn_sample(lams, sg, qb, kb, vb, cache_k, cache_v, batch, t, lam_init):
    hw = 2 * DIFF_HEAD_DIM
    past = cache_k.shape[1]
    blk = pl.BlockSpec((t, hw), lambda b, h: (b, h))
    cblk = pl.BlockSpec((None, past, hw), lambda b, h: (b, 0, h))
    small = pl.BlockSpec((1, DIFF_HEAD_DIM), lambda b, h: (0, 0))
    return pl.pallas_call(
        functools.partial(_diff_attn_sample_body, lam_init=lam_init),
        grid=(batch, N_DIFF_HEADS),
        in_specs=[small, small, small, small, pl.BlockSpec((1, hw), lambda b, h: (0, 0)), blk, blk, blk, cblk, cblk],
        out_specs=blk,
        out_shape=_sds((batch * t, DIFF_WIDTH), BF16),
        compiler_params=_cparams("arbitrary", "arbitrary"),
        name="diff_attn_sample",
    )(*lams, sg, qb, kb, vb, cache_k, cache_v)


CONV_HALO = 32
CONV_ROWS = 32


def _conv_tail_body(u_ref, prev_ref, pre_ref, w_ref, cb_ref, lg_ref, lb_ref, o_ref, ext_ref):
    tc = u_ref.shape[0]
    first = pl.program_id(1) == 0
    ext_ref[0:CONV_HALO, :] = jnp.where(first, pre_ref[...], prev_ref[...])
    ext_ref[CONV_HALO:CONV_HALO + tc, :] = u_ref[...]
    off = CONV_HALO - (CONV_WIDTH - 1)
    for r0 in range(0, tc, CONV_ROWS):
        acc = ext_ref[off + r0:off + r0 + CONV_ROWS, :] * w_ref[0:1, :]
        for j in range(1, CONV_WIDTH):
            acc = acc + ext_ref[off + r0 + j:off + r0 + j + CONV_ROWS, :] * w_ref[j:j + 1, :]
        c = acc + cb_ref[...]
        mu = jnp.mean(c, axis=-1, keepdims=True)
        var = jnp.mean(jnp.square(c - mu), axis=-1, keepdims=True)
        cn = (c - mu) * lax.rsqrt(var + EPS) * lg_ref[...] + lb_ref[...]
        o_ref[r0:r0 + CONV_ROWS, :] = (cn * jax.nn.sigmoid(cn)).astype(o_ref.dtype)


def _conv_tail(u, prefix, w, cb, lg, lb, batch, t):
    c = u.shape[1]
    tc = min(256, t)
    nt = t // tc
    hb = tc // CONV_HALO
    fix = lambda b, i: (0, 0)
    return pl.pallas_call(
        _conv_tail_body,
        grid=(batch, nt),
        in_specs=[pl.BlockSpec((tc, c), lambda b, i: (b * nt + i, 0)),
                  pl.BlockSpec((CONV_HALO, c), lambda b, i: (jnp.maximum((b * nt + i) * hb - 1, 0), 0)),
                  pl.BlockSpec((None, CONV_HALO, c), lambda b, i: (b, 0, 0)),
                  pl.BlockSpec((CONV_WIDTH, c), fix), pl.BlockSpec((1, c), fix), pl.BlockSpec((1, c), fix),
                  pl.BlockSpec((1, c), fix)],
        out_specs=pl.BlockSpec((tc, c), lambda b, i: (b * nt + i, 0)),
        out_shape=_sds((batch * t, c), BF16),
        scratch_shapes=[pltpu.VMEM((CONV_HALO + tc, c), F32)],
        compiler_params=_cparams("arbitrary", "arbitrary"),
        name="conv_tail",
    )(u, u, prefix, w, cb, lg, lb)


def _mix_out_body(x_ref, ao_ref, co_ref, wo_ref, g_ref, wq_ref, x1_ref, qc_ref, *, scale):
    dw = ao_ref.shape[1]
    x1 = (x_ref[...] + jnp.dot(ao_ref[...], wo_ref[0:dw, :], preferred_element_type=F32)
          + jnp.dot(co_ref[...], wo_ref[dw:, :], preferred_element_type=F32))
    x1_ref[...] = x1
    h = _rms(x1, g_ref[...]).astype(BF16)
    qc_ref[...] = (jnp.dot(h, wq_ref[...], preferred_element_type=F32) * scale).astype(BF16)


def _mix_out(x, ao, co, wo, g, wq, tm):
    n, d = x.shape
    row = lambda i: (i, 0)
    fix = lambda i: (0, 0)
    scale = (d // N_CROSS_HEADS) ** -0.5
    return pl.pallas_call(
        functools.partial(_mix_out_body, scale=scale),
        grid=(n // tm,),
        in_specs=[pl.BlockSpec((tm, d), row), pl.BlockSpec((tm, ao.shape[1]), row), pl.BlockSpec((tm, co.shape[1]), row),
                  pl.BlockSpec(wo.shape, fix), pl.BlockSpec((1, d), fix), pl.BlockSpec(wq.shape, fix)],
        out_specs=[pl.BlockSpec((tm, d), row), pl.BlockSpec((tm, d), row)],
        out_shape=[_sds((n, d), F32), _sds((n, d), BF16)],
        compiler_params=_cparams("arbitrary"),
        name="mix_out",
    )(x, ao, co, wo, g, wq)


def _mem_kv_body(m_ref, g_ref, wk_ref, wv_ref, mk_ref, mv_ref, mkb_ref, mvb_ref):
    h = _rms(m_ref[...], g_ref[...]).astype(BF16)
    mk = jnp.dot(h, wk_ref[...], preferred_element_type=F32)
    mv = jnp.dot(h, wv_ref[...], preferred_element_type=F32)
    mk_ref[...] = mk
    mv_ref[...] = mv
    mkb_ref[...] = mk.astype(BF16)
    mvb_ref[...] = mv.astype(BF16)


def _mem_kv(mem, g, wk, wv, tm):
    n, d = mem.shape
    row = lambda i: (i, 0)
    fix = lambda i: (0, 0)
    return pl.pallas_call(
        _mem_kv_body,
        grid=(n // tm,),
        in_specs=[pl.BlockSpec((tm, d), row), pl.BlockSpec((1, d), fix), pl.BlockSpec(wk.shape, fix),
                  pl.BlockSpec(wv.shape, fix)],
        out_specs=[pl.BlockSpec((tm, d), row)] * 4,
        out_shape=[_sds((n, d), F32), _sds((n, d), F32), _sds((n, d), BF16), _sds((n, d), BF16)],
        compiler_params=_cparams("arbitrary"),
        name="mem_kv",
    )(mem, g, wk, wv)


def _cross_attn_body(x1_ref, qc_ref, mk_ref, mv_ref, wo_ref, *rest, n_own):
    x2_ref = rest[-1]

    @pl.when(pl.program_id(0) < n_own)
    def _():
        hd = qc_ref.shape[1] // N_CROSS_HEADS
        outs = []
        for h in range(N_CROSS_HEADS):
            sl = slice(h * hd, (h + 1) * hd)
            s = lax.dot_general(qc_ref[:, sl], mk_ref[:, sl].astype(BF16), _NT, preferred_element_type=F32)
            (e,), r = _softmax_parts([s])
            outs.append(jnp.dot((e * r).astype(BF16), mv_ref[:, sl].astype(BF16), preferred_element_type=F32))
        o = jnp.concatenate(outs, axis=1).astype(BF16)
        x2_ref[...] = x1_ref[...] + jnp.dot(o, wo_ref[...], preferred_element_type=F32)

    if len(rest) == 2:
        @pl.when(pl.program_id(0) >= n_own)
        def _():
            x2_ref[...] = rest[0][...]


def _cross_attn(x1, qc, mk, mv, wo, tm, tiles_per_batch, tail=None):
    n, d = x1.shape
    m = mk.shape[1]
    n_own = n // tm
    n_tail = 0 if tail is None else tail.shape[0] // tm
    own = lambda i: (jnp.minimum(i, n_own - 1), 0)
    fix = lambda i: (0, 0)
    mem = pl.BlockSpec((None, m, d), lambda i: (jnp.minimum(i, n_own - 1) // tiles_per_batch, 0, 0))
    in_specs = [pl.BlockSpec((tm, d), own), pl.BlockSpec((tm, d), own), mem, mem, pl.BlockSpec(wo.shape, fix)]
    args = [x1, qc, mk, mv, wo]
    if tail is not None:
        in_specs.append(pl.BlockSpec((tm, d), lambda i: (jnp.maximum(i - n_own, 0), 0)))
        args.append(tail)
    return pl.pallas_call(
        functools.partial(_cross_attn_body, n_own=n_own),
        grid=(n_own + n_tail,),
        in_specs=in_specs,
        out_specs=pl.BlockSpec((tm, d), lambda i: (i, 0)),
        out_shape=_sds((n + n_tail * tm, d), F32),
        compiler_params=_cparams("arbitrary"),
        name="cross_attn",
    )(*args)


def _router_body(x_ref, g_ref, wr_ref, br_ref, hm_ref, d_ref, gt_ref, cnt_ref, brel_ref, tot_ref, cur_ref):
    @pl.when(pl.program_id(0) == 0)
    def _():
        cur_ref[...] = jnp.zeros_like(cur_ref)

    n_exp = wr_ref.shape[0]
    tb = x_ref.shape[0]
    hm = _rms(x_ref[...], g_ref[...])
    hm_ref[...] = hm.astype(BF16)
    logits = lax.dot_general(wr_ref[...], hm, _NT, precision=lax.Precision.HIGHEST,
                             preferred_element_type=F32) + br_ref[...]
    eidx = lax.broadcasted_iota(I32, (n_exp, tb), 0)
    work = logits
    sel = None
    top = None
    for k in range(TOP_K):
        m = jnp.max(work, axis=0, keepdims=True)
        top = m if k == 0 else top
        first = jnp.min(jnp.where(work == m, eidx, n_exp), axis=0, keepdims=True)
        pick = eidx == first
        sel = pick if k == 0 else jnp.logical_or(sel, pick)
        work = jnp.where(pick, -jnp.inf, work)
    e = jnp.where(sel, jnp.exp(logits - top), 0.0)
    gt_ref[...] = e * (1.0 / jnp.sum(e, axis=0, keepdims=True))
    self32 = jnp.where(sel, 1.0, 0.0)
    before = jnp.where(lax.broadcasted_iota(I32, (tb, tb), 0) < lax.broadcasted_iota(I32, (tb, tb), 1), 1.0, 0.0)
    rank = jnp.dot(self32.astype(BF16), before.astype(BF16), preferred_element_type=F32).astype(I32)
    d_ref[...] = jnp.where(sel, rank, NO_SLOT)
    cnt = jnp.broadcast_to(jnp.sum(self32, axis=1, keepdims=True).astype(I32), cnt_ref.shape)
    cnt_ref[...] = cnt
    brel_ref[...] = cur_ref[...]
    cur_ref[...] = cur_ref[...] + cnt
    tot_ref[...] = cur_ref[...]


def _router(x2, g, wr_t, br):
    n, d = x2.shape
    n_exp = wr_t.shape[0]
    tb = ROUTE_BLOCK
    nb = n // tb
    fix = lambda i: (0, 0)
    blk3 = lambda w: pl.BlockSpec((None, n_exp, w), lambda i: (i, 0, 0))
    return pl.pallas_call(
        _router_body,
        grid=(nb,),
        in_specs=[pl.BlockSpec((tb, d), lambda i: (i, 0)), pl.BlockSpec((1, d), fix), pl.BlockSpec((n_exp, d), fix),
                  pl.BlockSpec((n_exp, 1), fix)],
        out_specs=[pl.BlockSpec((tb, d), lambda i: (i, 0)), blk3(tb), blk3(tb), blk3(V7X_LANES), blk3(V7X_LANES),
                   pl.BlockSpec((n_exp, V7X_LANES), fix)],
        out_shape=[_sds((n, d), BF16), _sds((nb, n_exp, tb), I32), _sds((nb, n_exp, tb), F32),
                   _sds((nb, n_exp, V7X_LANES), I32), _sds((nb, n_exp, V7X_LANES), I32), _sds((n_exp, V7X_LANES), I32)],
        scratch_shapes=[pltpu.VMEM((n_exp, V7X_LANES), I32)],
        compiler_params=_cparams("arbitrary"),
        name="moe_router",
    )(x2, g, wr_t, br)


def _region_tiles(total):
    return (total + PIECE_ROWS + FFN_TILE - 1) // FFN_TILE


def _plan_body(tot_ref, start_ref, tile_e_ref, nt_ref, *, n_exp, max_tiles):
    shift = int(math.log2(FFN_TILE))

    def per_expert(e, pos):
        start_ref[e] = pos * FFN_TILE
        nt = lax.shift_right_logical(tot_ref[e] + (PIECE_ROWS + FFN_TILE - 1), shift)

        def mark(j, c):
            tile_e_ref[pos + j] = e
            return c

        lax.fori_loop(0, nt, mark, 0)
        return pos + nt

    n = lax.fori_loop(0, n_exp, per_expert, jnp.int32(0))
    nt_ref[0] = n

    def fill(j, c):
        tile_e_ref[j] = n_exp - 1
        return c

    lax.fori_loop(n, max_tiles, fill, 0)


def _plan(tot, max_tiles):
    n_exp = tot.shape[0]
    smem = pl.BlockSpec(memory_space=pltpu.SMEM)
    return pl.pallas_call(
        functools.partial(_plan_body, n_exp=n_exp, max_tiles=max_tiles),
        in_specs=[smem],
        out_specs=[smem, smem, smem],
        out_shape=[_sds((n_exp,), I32), _sds((max_tiles,), I32), _sds((1,), I32)],
        name="moe_plan",
    )(tot)


def _window(start_ref, brel_ref, cnt_ref, b, e, n_exp):
    base = start_ref[e] + brel_ref[b * n_exp + e]
    return base, jnp.bitwise_and(base, GROUP - 1), cnt_ref[b * n_exp + e]


def _num_rounds(start_ref, brel_ref, cnt_ref, b, n_exp):
    shift = int(math.log2(PIECE_ROWS))

    def f(e, m):
        _, s, n = _window(start_ref, brel_ref, cnt_ref, b, e, n_exp)
        return jnp.maximum(m, lax.shift_right_logical(s + n + (PIECE_ROWS - 1), shift))

    return lax.fori_loop(0, n_exp, f, jnp.int32(1))


def _perm_rows(d_ref, start_ref, brel_ref, cnt_ref, b, r, e, n_exp):
    tb = d_ref.shape[1]
    _, s, _ = _window(start_ref, brel_ref, cnt_ref, b, e, n_exp)
    tgt = d_ref[e:e + 1, :] + (s - PIECE_ROWS * r)
    return lax.broadcasted_iota(I32, (PIECE_ROWS, tb), 0) == tgt


EXPERTS_PER_DOT = 8


def _dispatch_body(brel_ref, cnt_ref, start_ref, tot_ref, hm_ref, d_ref, xs_ref, pieces_ref, carry_ref, zero_ref,
                   sem, *, n_exp):
    b = pl.program_id(0)
    p_rows, grp = PIECE_ROWS, GROUP
    d = hm_ref.shape[1]
    win = functools.partial(_window, start_ref, brel_ref, cnt_ref, b, n_exp=n_exp)

    @pl.when(b == 0)
    def _():
        carry_ref[...] = jnp.zeros_like(carry_ref)
        zero_ref[...] = jnp.zeros_like(zero_ref)

        def n_fill(e):
            region_end = start_ref[e] + _region_tiles(tot_ref[e]) * FFN_TILE
            z0 = start_ref[e] + jnp.bitwise_and(tot_ref[e] + (grp - 1), -grp)
            return z0, lax.shift_right_logical(region_end - z0 + (p_rows - 1), int(math.log2(p_rows)))

        def fill_copy(e, q):
            z0, _ = n_fill(e)
            return pltpu.make_async_copy(zero_ref.at[pl.ds(0, p_rows)],
                                         xs_ref.at[pl.ds(pl.multiple_of(z0 + q * p_rows, grp), p_rows)], sem.at[0])

        def start_e(e, c):
            lax.fori_loop(0, n_fill(e)[1], lambda q, c2: (fill_copy(e, q).start(), c2)[1], 0)
            return c

        def wait_e(e, c):
            lax.fori_loop(0, n_fill(e)[1], lambda q, c2: (fill_copy(e, q).wait(), c2)[1], 0)
            return c

        lax.fori_loop(0, n_exp, start_e, 0)
        lax.fori_loop(0, n_exp, wait_e, 0)

        used = lax.shift_right_logical(start_ref[n_exp - 1], int(math.log2(FFN_TILE))) + _region_tiles(tot_ref[n_exp - 1])
        total_tiles = (xs_ref.shape[0] - p_rows) // FFN_TILE

        def tail_copy(t):
            return pltpu.make_async_copy(zero_ref, xs_ref.at[pl.ds(pl.multiple_of(t * FFN_TILE, FFN_TILE), FFN_TILE)],
                                         sem.at[0])

        slack_copy = pltpu.make_async_copy(zero_ref.at[pl.ds(0, p_rows)],
                                           xs_ref.at[pl.ds(total_tiles * FFN_TILE, p_rows)], sem.at[0])
        lax.fori_loop(used, total_tiles, lambda t, c: (tail_copy(t).start(), c)[1], 0)
        slack_copy.start()
        lax.fori_loop(used, total_tiles, lambda t, c: (tail_copy(t).wait(), c)[1], 0)
        slack_copy.wait()

    def piece_copy(e, r):
        base, s, _ = win(e)
        dst = pl.multiple_of(base - s + r * p_rows, grp)
        return pltpu.make_async_copy(pieces_ref.at[pl.ds(pl.multiple_of(e * p_rows, p_rows), p_rows)],
                                     xs_ref.at[pl.ds(dst, p_rows)], sem.at[0])

    def needed(e, r):
        _, s, n = win(e)
        return jnp.logical_or(r == 0, s + n > r * p_rows)

    def one_round(r, c):
        for g0 in range(0, n_exp, EXPERTS_PER_DOT):
            pm = jnp.concatenate(
                [jnp.where(_perm_rows(d_ref, start_ref, brel_ref, cnt_ref, b, r, e, n_exp), 1.0, 0.0).astype(BF16)
                 for e in range(g0, g0 + EXPERTS_PER_DOT)], axis=0)
            pieces_ref[g0 * p_rows:(g0 + EXPERTS_PER_DOT) * p_rows, :] = jnp.dot(
                pm, hm_ref[...], preferred_element_type=F32).astype(BF16)

        def merge_carry(e, c2):
            _, s, n = win(e)
            rows = pl.ds(pl.multiple_of(e * p_rows, grp), grp)
            crow = pl.ds(pl.multiple_of(e * grp, grp), grp)

            @pl.when(r == 0)
            def _():
                keep = lax.broadcasted_iota(I32, (grp, d), 0) < s
                pieces_ref[rows, :] = jnp.where(keep, carry_ref[crow, :], pieces_ref[rows, :])

                @pl.when(s + n >= p_rows)
                def _():
                    carry_ref[crow, :] = jnp.zeros((grp, d), carry_ref.dtype)

            @pl.when(lax.shift_right_logical(s + n, int(math.log2(p_rows))) == r)
            def _():
                g = lax.shift_right_logical(jnp.bitwise_and(s + n, p_rows - 1), int(math.log2(grp)))
                carry_ref[crow, :] = pieces_ref[pl.ds(pl.multiple_of(e * p_rows + g * grp, grp), grp), :]

            return c2

        lax.fori_loop(0, n_exp, merge_carry, 0)

        def start_e(e, c2):
            @pl.when(needed(e, r))
            def _():
                piece_copy(e, r).start()
            return c2

        def wait_e(e, c2):
            @pl.when(needed(e, r))
            def _():
                piece_copy(e, r).wait()
            return c2

        lax.fori_loop(0, n_exp, start_e, 0)
        lax.fori_loop(0, n_exp, wait_e, 0)
        return c

    lax.fori_loop(0, _num_rounds(start_ref, brel_ref, cnt_ref, b, n_exp), one_round, 0)


def _dispatch(brel, cnt, start, tot, hm, dmat, rows_max):
    n, d = hm.shape
    nb, n_exp, tb = dmat.shape
    grid_spec = pltpu.PrefetchScalarGridSpec(
        num_scalar_prefetch=4,
        grid=(nb,),
        in_specs=[pl.BlockSpec((tb, d), lambda i, *_: (i, 0)), pl.BlockSpec((None, n_exp, tb), lambda i, *_: (i, 0, 0))],
        out_specs=pl.BlockSpec(memory_space=pl.ANY),
        scratch_shapes=[pltpu.VMEM((n_exp * PIECE_ROWS, d), BF16), pltpu.VMEM((n_exp * GROUP, d), BF16),
                        pltpu.VMEM((FFN_TILE, d), BF16), pltpu.SemaphoreType.DMA((1,))],
    )
    return pl.pallas_call(
        functools.partial(_dispatch_body, n_exp=n_exp),
        grid_spec=grid_spec,
        out_shape=_sds((rows_max, d), BF16),
        compiler_params=_cparams("arbitrary"),
        name="moe_dispatch",
    )(brel, cnt, start, tot, hm, dmat)


def _combine_body(brel_ref, cnt_ref, start_ref, x2_ref, d_ref, gt_ref, gf_ref, ys_ref, yp_ref, ysm_ref, pieces_ref,
                  acc_ref, sem, *, n_exp, nb_prompt, final):
    b = pl.program_id(0)
    p_rows, grp = PIECE_ROWS, GROUP
    win = functools.partial(_window, start_ref, brel_ref, cnt_ref, b, n_exp=n_exp)
    acc_ref[...] = x2_ref[...]

    def piece_copy(e, r):
        base, s, _ = win(e)
        src = pl.multiple_of(base - s + r * p_rows, grp)
        return pltpu.make_async_copy(ys_ref.at[pl.ds(src, p_rows)],
                                     pieces_ref.at[pl.ds(pl.multiple_of(e * p_rows, p_rows), p_rows)], sem.at[0])

    def needed(e, r):
        _, s, n = win(e)
        return jnp.logical_or(r == 0, s + n > r * p_rows)

    def one_round(r, c):
        def start_e(e, c2):
            @pl.when(needed(e, r))
            def _():
                piece_copy(e, r).start()
            return c2

        def wait_e(e, c2):
            @pl.when(needed(e, r))
            def _():
                piece_copy(e, r).wait()
            return c2

        lax.fori_loop(0, n_exp, start_e, 0)
        lax.fori_loop(0, n_exp, wait_e, 0)
        for g0 in range(0, n_exp, EXPERTS_PER_DOT):
            pms, ys = [], []
            for e in range(g0, g0 + EXPERTS_PER_DOT):
                hit = _perm_rows(d_ref, start_ref, brel_ref, cnt_ref, b, r, e, n_exp)
                gate = jnp.sum(jnp.where(hit, gt_ref[e:e + 1, :], 0.0), axis=1, keepdims=True)
                pms.append(jnp.where(hit, 1.0, 0.0).astype(BF16))
                ys.append((pieces_ref[e * p_rows:(e + 1) * p_rows, :].astype(F32) * gate).astype(BF16))
            acc_ref[...] += lax.dot_general(jnp.concatenate(pms, axis=0), jnp.concatenate(ys, axis=0), _TN,
                                            preferred_element_type=F32)
        return c

    lax.fori_loop(0, _num_rounds(start_ref, brel_ref, cnt_ref, b, n_exp), one_round, 0)
    y = _rms(acc_ref[...], gf_ref[...]) if final else acc_ref[...]

    @pl.when(b < nb_prompt)
    def _():
        yp_ref[...] = y

    @pl.when(b >= nb_prompt)
    def _():
        ysm_ref[...] = y


def _combine(brel, cnt, start, x2, dmat, gates, gf, ys, n_prompt, final):
    n, d = x2.shape
    nb, n_exp, tb = dmat.shape
    nbp = n_prompt // tb
    blk3 = pl.BlockSpec((None, n_exp, tb), lambda i, *_: (i, 0, 0))
    grid_spec = pltpu.PrefetchScalarGridSpec(
        num_scalar_prefetch=3,
        grid=(nb,),
        in_specs=[pl.BlockSpec((tb, d), lambda i, *_: (i, 0)), blk3, blk3, pl.BlockSpec((1, d), lambda i, *_: (0, 0)),
                  pl.BlockSpec(memory_space=pl.ANY)],
        out_specs=[pl.BlockSpec((tb, d), lambda i, *_: (jnp.minimum(i, nbp - 1), 0)),
                   pl.BlockSpec((tb, d), lambda i, *_: (jnp.maximum(i - nbp, 0), 0))],
        scratch_shapes=[pltpu.VMEM((n_exp * PIECE_ROWS, d), BF16), pltpu.VMEM((tb, d), F32),
                        pltpu.SemaphoreType.DMA((1,))],
    )
    return pl.pallas_call(
        functools.partial(_combine_body, n_exp=n_exp, nb_prompt=nbp, final=final),
        grid_spec=grid_spec,
        out_shape=[_sds((n_prompt, d), F32), _sds((n - n_prompt, d), F32)],
        compiler_params=_cparams("arbitrary"),
        name="moe_combine",
    )(brel, cnt, start, x2, dmat, gates, gf, ys)


def _ffn_body(tile_e_ref, nt_ref, x_ref, wg_ref, wu_ref, bg_ref, bu_ref, wd_ref, bd_ref, y_ref):
    @pl.when(pl.program_id(0) >= nt_ref[0])
    def _():
        y_ref[...] = jnp.zeros_like(y_ref)

    @pl.when(pl.program_id(0) < nt_ref[0])
    def _():
        x = x_ref[...]
        gate = jnp.minimum(jnp.dot(x, wg_ref[...], preferred_element_type=F32) + bg_ref[...], SWIGLU_LIMIT)
        up = jnp.clip(jnp.dot(x, wu_ref[...], preferred_element_type=F32) + bu_ref[...], -SWIGLU_LIMIT, SWIGLU_LIMIT)
        act = (up + 1.0) * gate * jax.nn.sigmoid(SWIGLU_ALPHA * gate)
        y_ref[...] = (jnp.dot(act.astype(BF16), wd_ref[...], preferred_element_type=F32) + bd_ref[...]).astype(y_ref.dtype)


def _ffn(tile_e, nt, xs, wg, wu, bg, bu, wd, bd):
    d = xs.shape[1]
    f = wg.shape[2]
    max_tiles = tile_e.shape[0]
    tile = lambda i, te, nt: (jnp.minimum(i, nt[0] - 1), 0)
    wsel = lambda i, te, nt: (te[i], 0, 0)
    grid_spec = pltpu.PrefetchScalarGridSpec(
        num_scalar_prefetch=2,
        grid=(max_tiles,),
        in_specs=[pl.BlockSpec((FFN_TILE, d), tile), pl.BlockSpec((None, d, f), wsel), pl.BlockSpec((None, d, f), wsel),
                  pl.BlockSpec((None, 1, f), wsel), pl.BlockSpec((None, 1, f), wsel), pl.BlockSpec((None, f, d), wsel),
                  pl.BlockSpec((None, 1, d), wsel)],
        out_specs=pl.BlockSpec((FFN_TILE, d), lambda i, te, nt: (i, 0)),
    )
    return pl.pallas_call(
        _ffn_body,
        grid_spec=grid_spec,
        out_shape=_sds((max_tiles * FFN_TILE, d), BF16),
        compiler_params=_cparams("arbitrary"),
        name="moe_ffn",
    )(tile_e, nt, xs, wg, wu, bg, bu, wd, bd)


def _moe(x2, n_prompt, g_moe, wr_t, br, wg, wu, bg, bu, wd, bd, g_final, final):
    n, d = x2.shape
    n_exp = wr_t.shape[0]
    hm, dmat, gates, cnt, brel, tot = _router(x2, g_moe, wr_t, br)
    cnt = cnt[:, :, 0].reshape(-1)
    brel = brel[:, :, 0].reshape(-1)
    tot = tot[:, 0]
    max_tiles = (TOP_K * n + n_exp * (PIECE_ROWS + FFN_TILE - 1)) // FFN_TILE + 1
    start, tile_e, nt = _plan(tot, max_tiles)
    rows_max = max_tiles * FFN_TILE + PIECE_ROWS
    xs = _dispatch(brel, cnt, start, tot, hm, dmat, rows_max)
    ys = _ffn(tile_e, nt, xs, wg, wu, bg, bu, wd, bd)
    return _combine(brel, cnt, start, x2, dmat, gates, g_final, ys, n_prompt, final)


def _lambda_init(layer):
    return 0.8 - 0.6 * math.exp(-0.3 * layer)


def kernel(x_prompt, x_sample, cache_k, cache_v, state_conv, cache_mem_k, cache_mem_v, mem_prompt, norm_mix, w_in, lambda_q1, lambda_k1, lambda_q2, lambda_k2, subln_g, conv_w, conv_b, conv_ln_g, conv_ln_b, w_out, norm_cross, norm_mem, w_cq, w_ck, w_cv, w_co, norm_moe, w_router, b_router, w_gu, b_gu, w_down, b_down, norm_final):
    depth = w_in.shape[0]
    bp, sp, d = x_prompt.shape
    bs, ts, _ = x_sample.shape
    n_p, n_s = bp * sp, bs * ts
    n_mem = mem_prompt.shape[1]
    n_exp = w_router.shape[2]
    cc = conv_w.shape[2]
    halo = CONV_WIDTH - 1
    row = lambda a: a.reshape(1, -1)

    xp = x_prompt.reshape(n_p, d)
    xs = x_sample.reshape(n_s, d)
    k_p, v_p, cs_p, mk_p, mv_p, k_s, v_s, cs_s = [], [], [], [], [], [], [], []
    for l in range(depth):
        lam_init = _lambda_init(l)
        lams = [row(lambda_q1[l]), row(lambda_k1[l]), row(lambda_q2[l]), row(lambda_k2[l])]
        sg = row(subln_g[l])
        w_in_b = w_in[l].astype(BF16)
        w_out_b = w_out[l].astype(BF16)
        w_cq_b, w_ck_b, w_cv_b, w_co_b = (w[l].astype(BF16) for w in (w_cq, w_ck, w_cv, w_co))
        conv_args = (conv_w[l], row(conv_b[l]), row(conv_ln_g[l]), row(conv_ln_b[l]))

        k, v, u, qb, kb, vb = _in_proj(xp, row(norm_mix[l]), w_in_b, 512)
        ao = _diff_attn_prompt(lams, sg, qb, kb, vb, bp, sp, lam_init)
        co = _conv_tail(u, jnp.zeros((bp, CONV_HALO, cc), F32), *conv_args, bp, sp)
        x1, qc = _mix_out(xp, ao, co, w_out_b, row(norm_cross[l]), w_cq_b, 512)
        mk, mv, mkb, mvb = _mem_kv(mem_prompt.reshape(bp * n_mem, d), row(norm_mem[l]), w_ck_b, w_cv_b, n_mem)
        tq = min(256, sp)
        x1_p, qc_p, mk_b, mv_b = x1, qc, mkb.reshape(bp, n_mem, d), mvb.reshape(bp, n_mem, d)
        k_p.append(k.reshape(bp, sp, N_DIFF_HEADS, 2 * DIFF_HEAD_DIM))
        v_p.append(v.reshape(bp, sp, N_DIFF_HEADS, 2 * DIFF_HEAD_DIM))
        cs_p.append(u.reshape(bp, sp, cc)[:, sp - halo:])
        mk_p.append(mk.reshape(bp, n_mem, N_CROSS_HEADS, d // N_CROSS_HEADS))
        mv_p.append(mv.reshape(bp, n_mem, N_CROSS_HEADS, d // N_CROSS_HEADS))

        k, v, u, qb, kb, vb = _in_proj(xs, row(norm_mix[l]), w_in_b, min(512, n_s))
        past = cache_k.shape[2]
        ao = _diff_attn_sample(lams, sg, qb, kb, vb, cache_k[l].reshape(bs, past, DIFF_WIDTH),
                               cache_v[l].reshape(bs, past, DIFF_WIDTH), bs, ts, lam_init)
        prefix = jnp.pad(state_conv[l], ((0, 0), (CONV_HALO - halo, 0), (0, 0)))
        co = _conv_tail(u, prefix, *conv_args, bs, ts)
        x1, qc = _mix_out(xs, ao, co, w_out_b, row(norm_cross[l]), w_cq_b, min(512, n_s))
        x2_s = _cross_attn(x1, qc, cache_mem_k[l].reshape(bs, n_mem, d), cache_mem_v[l].reshape(bs, n_mem, d), w_co_b,
                           ts, 1)
        joint = _cross_attn(x1_p, qc_p, mk_b, mv_b, w_co_b, tq, sp // tq, tail=x2_s)
        k_s.append(k.reshape(bs, ts, N_DIFF_HEADS, 2 * DIFF_HEAD_DIM))
        v_s.append(v.reshape(bs, ts, N_DIFF_HEADS, 2 * DIFF_HEAD_DIM))
        u_ext = jnp.concatenate([state_conv[l], u.reshape(bs, ts, cc)], axis=1)
        cs_s.append(u_ext[:, u_ext.shape[1] - halo:])

        wg = w_gu[l][:, :, 0::2].astype(BF16)
        wu = w_gu[l][:, :, 1::2].astype(BF16)
        bg = b_gu[l][:, 0::2].reshape(n_exp, 1, -1)
        bu = b_gu[l][:, 1::2].reshape(n_exp, 1, -1)
        yp, ysm = _moe(joint, n_p, row(norm_moe[l]), w_router[l].T, b_router[l].reshape(n_exp, 1), wg, wu, bg, bu,
                       w_down[l].astype(BF16), b_down[l].reshape(n_exp, 1, d), row(norm_final), l == depth - 1)
        xp, xs = yp, ysm

    st = lambda xs_: jnp.stack(xs_)
    return (xp.reshape(bp, sp, d), xs.reshape(bs, ts, d), st(k_p), st(v_p), st(cs_p), st(mk_p), st(mv_p), st(k_s),
            st(v_s), st(cs_s))
```

```python
import functools
import math

import jax
import jax.numpy as jnp
from jax import lax
from jax.experimental import pallas as pl
from jax.experimental.pallas import tpu as pltpu

F32 = jnp.float32
BF16 = jnp.bfloat16
I32 = jnp.int32

EPS = 1e-6
CHUNK = 64
N_DIFF_HEADS = 4
DIFF_HEAD_DIM = 64
DIFF_WIDTH = N_DIFF_HEADS * 2 * DIFF_HEAD_DIM
CONV_WIDTH = 31
N_CROSS_HEADS = 4
TOP_K = 4
SWIGLU_LIMIT = 7.0
SWIGLU_ALPHA = 1.702
NEG_INF = -1e30

V7X_LANES = 128
V7X_BF16_ROWS_PER_TILE = 16
V7X_VMEM_LIMIT_BYTES = 56 * 1024 * 1024

ROUTE_BLOCK = 256
PIECE_ROWS = 64
GROUP = V7X_BF16_ROWS_PER_TILE
FFN_TILE = 512
NO_SLOT = -(1 << 20)

_NT = (((1,), (1,)), ((), ()))
_TN = (((0,), (0,)), ((), ()))


def _cparams(*sem):
    return pltpu.CompilerParams(dimension_semantics=sem, vmem_limit_bytes=V7X_VMEM_LIMIT_BYTES)


def _sds(shape, dtype):
    return jax.ShapeDtypeStruct(shape, dtype)


def _rms(x, g):
    return x * lax.rsqrt(jnp.mean(x * x, axis=-1, keepdims=True) + EPS) * g


def _softmax_parts(pieces):
    m = functools.reduce(jnp.maximum, [jnp.max(s, axis=-1, keepdims=True) for s in pieces])
    es = [jnp.exp(s - m) for s in pieces]
    tot = functools.reduce(jnp.add, [jnp.sum(e, axis=-1, keepdims=True) for e in es])
    return es, 1.0 / tot


def _in_proj_body(x_ref, g_ref, w_ref, k_ref, v_ref, u_ref, qb_ref, kb_ref, vb_ref, *, cc):
    h = _rms(x_ref[...], g_ref[...]).astype(BF16)

    def proj(lo, n):
        return jnp.dot(h, w_ref[:, lo:lo + n], preferred_element_type=F32)

    dw = DIFF_WIDTH
    qb_ref[...] = (proj(0, dw) * (DIFF_HEAD_DIM ** -0.5)).astype(BF16)
    k = proj(dw, dw)
    k_ref[...] = k
    kb_ref[...] = k.astype(BF16)
    v = proj(2 * dw, dw)
    v_ref[...] = v
    vb_ref[...] = v.astype(BF16)
    u_ref[...] = proj(3 * dw, cc) * jax.nn.sigmoid(proj(3 * dw + cc, cc))


def _in_proj(x, g, w, tm):
    n, d = x.shape
    cols = w.shape[1]
    cc = (cols - 3 * DIFF_WIDTH) // 2
    row = lambda i: (i, 0)
    fix = lambda i: (0, 0)
    return pl.pallas_call(
        functools.partial(_in_proj_body, cc=cc),
        grid=(n // tm,),
        in_specs=[pl.BlockSpec((tm, d), row), pl.BlockSpec((1, d), fix), pl.BlockSpec((d, cols), fix)],
        out_specs=[pl.BlockSpec((tm, DIFF_WIDTH), row), pl.BlockSpec((tm, DIFF_WIDTH), row), pl.BlockSpec((tm, cc), row),
                   pl.BlockSpec((tm, DIFF_WIDTH), row), pl.BlockSpec((tm, DIFF_WIDTH), row),
                   pl.BlockSpec((tm, DIFF_WIDTH), row)],
        out_shape=[_sds((n, DIFF_WIDTH), F32), _sds((n, DIFF_WIDTH), F32), _sds((n, cc), F32),
                   _sds((n, DIFF_WIDTH), BF16), _sds((n, DIFF_WIDTH), BF16), _sds((n, DIFF_WIDTH), BF16)],
        compiler_params=_cparams("arbitrary"),
        name="in_proj",
    )(x, g, w)


def _diff_lambda(lq1, lk1, lq2, lk2, lam_init):
    return (jnp.exp(jnp.sum(lq1[...] * lk1[...], axis=-1, keepdims=True))
            - jnp.exp(jnp.sum(lq2[...] * lk2[...], axis=-1, keepdims=True)) + lam_init)


def _split_components(q):
    lane = lax.broadcasted_iota(I32, q.shape, 1)
    zero = jnp.zeros_like(q)
    return jnp.where(lane < DIFF_HEAD_DIM, q, zero), jnp.where(lane >= DIFF_HEAD_DIM, q, zero)


def _sub_ln(o, sg, lam_init):
    o = o * lax.rsqrt(jnp.mean(o * o, axis=-1, keepdims=True) + EPS)
    return o * sg * (1.0 - lam_init)


def _diff_attn_prompt_body(lq1, lk1, lq2, lk2, sg_ref, q_ref, k_ref, v_ref, o_ref, *, tq, lam_init):
    lam = _diff_lambda(lq1, lk1, lq2, lk2, lam_init)
    seq = q_ref.shape[0]
    vis = (lax.broadcasted_iota(I32, (tq, tq), 1) // CHUNK) <= (lax.broadcasted_iota(I32, (tq, tq), 0) // CHUNK)
    for i in range(seq // tq):
        lo, ext = i * tq, (i + 1) * tq
        es, rs = [], []
        for qc in _split_components(q_ref[lo:ext, :]):
            s_diag = lax.dot_general(qc, k_ref[lo:ext, :], _NT, preferred_element_type=F32)
            pieces = [jnp.where(vis, s_diag, NEG_INF)]
            if i > 0:
                pieces.insert(0, lax.dot_general(qc, k_ref[0:lo, :], _NT, preferred_element_type=F32))
            e, r = _softmax_parts(pieces)
            es.append(e[0] if len(e) == 1 else jnp.concatenate(e, axis=1))
            rs.append(r)
        a = es[0] * rs[0] - es[1] * (lam * rs[1])
        o = jnp.dot(a.astype(BF16), v_ref[0:ext, :], preferred_element_type=F32)
        o_ref[lo:ext, :] = _sub_ln(o, sg_ref[...], lam_init).astype(o_ref.dtype)


def _diff_attn_prompt(lams, sg, qb, kb, vb, batch, seq, lam_init):
    hw = 2 * DIFF_HEAD_DIM
    tq = min(256, seq)
    blk = pl.BlockSpec((seq, hw), lambda b, h: (b, h))
    small = pl.BlockSpec((1, DIFF_HEAD_DIM), lambda b, h: (0, 0))
    return pl.pallas_call(
        functools.partial(_diff_attn_prompt_body, tq=tq, lam_init=lam_init),
        grid=(batch, N_DIFF_HEADS),
        in_specs=[small, small, small, small, pl.BlockSpec((1, hw), lambda b, h: (0, 0)), blk, blk, blk],
        out_specs=blk,
        out_shape=_sds((batch * seq, DIFF_WIDTH), BF16),
        compiler_params=_cparams("arbitrary", "arbitrary"),
        name="diff_attn_prompt",
    )(*lams, sg, qb, kb, vb)


def _diff_attn_sample_body(lq1, lk1, lq2, lk2, sg_ref, q_ref, k_ref, v_ref, ck_ref, cv_ref, o_ref, *, lam_init):
    lam = _diff_lambda(lq1, lk1, lq2, lk2, lam_init)
    t = q_ref.shape[0]
    past = ck_ref.shape[0]
    ck = ck_ref[...].astype(BF16)
    q_chunk = (past + lax.broadcasted_iota(I32, (t, 1), 0)) // CHUNK
    vis_past = (lax.broadcasted_iota(I32, (t, past), 1) // CHUNK) <= q_chunk
    vis_new = ((past + lax.broadcasted_iota(I32, (t, t), 1)) // CHUNK) <= q_chunk
    es, rs = [], []
    for qc in _split_components(q_ref[...]):
        s_past = lax.dot_general(qc, ck, _NT, preferred_element_type=F32)
        s_new = lax.dot_general(qc, k_ref[...], _NT, preferred_element_type=F32)
        e, r = _softmax_parts([jnp.where(vis_past, s_past, NEG_INF), jnp.where(vis_new, s_new, NEG_INF)])
        es.append(e)
        rs.append(r)
    a_past = es[0][0] * rs[0] - es[1][0] * (lam * rs[1])
    a_new = es[0][1] * rs[0] - es[1][1] * (lam * rs[1])
    o = (jnp.dot(a_past.astype(BF16), cv_ref[...].astype(BF16), preferred_element_type=F32)
         + jnp.dot(a_new.astype(BF16), v_ref[...], preferred_element_type=F32))
    o_ref[...] = _sub_ln(o, sg_ref[...], lam_init).astype(o_ref.dtype)


def _diff_attn_sample(lams, sg, qb, kb, vb, cache_k, cache_v, batch, t, lam_init):
    hw = 2 * DIFF_HEAD_DIM
    past = cache_k.shape[1]
    blk = pl.BlockSpec((t, hw), lambda b, h: (b, h))
    cblk = pl.BlockSpec((None, past, hw), lambda b, h: (b, 0, h))
    small = pl.BlockSpec((1, DIFF_HEAD_DIM), lambda b, h: (0, 0))
    return pl.pallas_call(
        functools.partial(_diff_attn_sample_body, lam_init=lam_init),
        grid=(batch, N_DIFF_HEADS),
        in_specs=[small, small, small, small, pl.BlockSpec((1, hw), lambda b, h: (0, 0)), blk, blk, blk, cblk, cblk],
        out_specs=blk,
        out_shape=_sds((batch * t, DIFF_WIDTH), BF16),
        compiler_params=_cparams("arbitrary", "arbitrary"),
        name="diff_attn_sample",
    )(*lams, sg, qb, kb, vb, cache_k, cache_v)


CONV_HALO = 32
CONV_ROWS = 32


def _conv_tail_body(u_ref, prev_ref, pre_ref, w_ref, cb_ref, lg_ref, lb_ref, o_ref, ext_ref):
    tc = u_ref.shape[0]
    first = pl.program_id(1) == 0
    ext_ref[0:CONV_HALO, :] = jnp.where(first, pre_ref[...], prev_ref[...])
    ext_ref[CONV_HALO:CONV_HALO + tc, :] = u_ref[...]
    off = CONV_HALO - (CONV_WIDTH - 1)
    for r0 in range(0, tc, CONV_ROWS):
        acc = ext_ref[off + r0:off + r0 + CONV_ROWS, :] * w_ref[0:1, :]
        for j in range(1, CONV_WIDTH):
            acc = acc + ext_ref[off + r0 + j:off + r0 + j + CONV_ROWS, :] * w_ref[j:j + 1, :]
        c = acc + cb_ref[...]
        mu = jnp.mean(c, axis=-1, keepdims=True)
        var = jnp.mean(jnp.square(c - mu), axis=-1, keepdims=True)
        cn = (c - mu) * lax.rsqrt(var + EPS) * lg_ref[...] + lb_ref[...]
        o_ref[r0:r0 + CONV_ROWS, :] = (cn * jax.nn.sigmoid(cn)).astype(o_ref.dtype)


def _conv_tail(u, prefix, w, cb, lg, lb, batch, t):
    c = u.shape[1]
    tc = min(256, t)
    nt = t // tc
    hb = tc // CONV_HALO
    fix = lambda b, i: (0, 0)
    return pl.pallas_call(
        _conv_tail_body,
        grid=(batch, nt),
        in_specs=[pl.BlockSpec((tc, c), lambda b, i: (b * nt + i, 0)),
                  pl.BlockSpec((CONV_HALO, c), lambda b, i: (jnp.maximum((b * nt + i) * hb - 1, 0), 0)),
                  pl.BlockSpec((None, CONV_HALO, c), lambda b, i: (b, 0, 0)),
                  pl.BlockSpec((CONV_WIDTH, c), fix), pl.BlockSpec((1, c), fix), pl.BlockSpec((1, c), fix),
                  pl.BlockSpec((1, c), fix)],
        out_specs=pl.BlockSpec((tc, c), lambda b, i: (b * nt + i, 0)),
        out_shape=_sds((batch * t, c), BF16),
        scratch_shapes=[pltpu.VMEM((CONV_HALO + tc, c), F32)],
        compiler_params=_cparams("arbitrary", "arbitrary"),
        name="conv_tail",
    )(u, u, prefix, w, cb, lg, lb)


def _mix_out_body(x_ref, ao_ref, co_ref, wo_ref, g_ref, wq_ref, x1_ref, qc_ref, *, scale):
    dw = ao_ref.shape[1]
    x1 = (x_ref[...] + jnp.dot(ao_ref[...], wo_ref[0:dw, :], preferred_element_type=F32)
          + jnp.dot(co_ref[...], wo_ref[dw:, :], preferred_element_type=F32))
    x1_ref[...] = x1
    h = _rms(x1, g_ref[...]).astype(BF16)
    qc_ref[...] = (jnp.dot(h, wq_ref[...], preferred_element_type=F32) * scale).astype(BF16)


def _mix_out(x, ao, co, wo, g, wq, tm):
    n, d = x.shape
    row = lambda i: (i, 0)
    fix = lambda i: (0, 0)
    scale = (d // N_CROSS_HEADS) ** -0.5
    return pl.pallas_call(
        functools.partial(_mix_out_body, scale=scale),
        grid=(n // tm,),
        in_specs=[pl.BlockSpec((tm, d), row), pl.BlockSpec((tm, ao.shape[1]), row), pl.BlockSpec((tm, co.shape[1]), row),
                  pl.BlockSpec(wo.shape, fix), pl.BlockSpec((1, d), fix), pl.BlockSpec(wq.shape, fix)],
        out_specs=[pl.BlockSpec((tm, d), row), pl.BlockSpec((tm, d), row)],
        out_shape=[_sds((n, d), F32), _sds((n, d), BF16)],
        compiler_params=_cparams("arbitrary"),
        name="mix_out",
    )(x, ao, co, wo, g, wq)


def _mem_kv_body(m_ref, g_ref, wk_ref, wv_ref, mk_ref, mv_ref, mkb_ref, mvb_ref):
    h = _rms(m_ref[...], g_ref[...]).astype(BF16)
    mk = jnp.dot(h, wk_ref[...], preferred_element_type=F32)
    mv = jnp.dot(h, wv_ref[...], preferred_element_type=F32)
    mk_ref[...] = mk
    mv_ref[...] = mv
    mkb_ref[...] = mk.astype(BF16)
    mvb_ref[...] = mv.astype(BF16)


def _mem_kv(mem, g, wk, wv, tm):
    n, d = mem.shape
    row = lambda i: (i, 0)
    fix = lambda i: (0, 0)
    return pl.pallas_call(
        _mem_kv_body,
        grid=(n // tm,),
        in_specs=[pl.BlockSpec((tm, d), row), pl.BlockSpec((1, d), fix), pl.BlockSpec(wk.shape, fix),
                  pl.BlockSpec(wv.shape, fix)],
        out_specs=[pl.BlockSpec((tm, d), row)] * 4,
        out_shape=[_sds((n, d), F32), _sds((n, d), F32), _sds((n, d), BF16), _sds((n, d), BF16)],
        compiler_params=_cparams("arbitrary"),
        name="mem_kv",
    )(mem, g, wk, wv)


def _cross_attn_body(x1_ref, qc_ref, mk_ref, mv_ref, wo_ref, *rest, n_own):
    x2_ref = rest[-1]

    @pl.when(pl.program_id(0) < n_own)
    def _():
        hd = qc_ref.shape[1] // N_CROSS_HEADS
        outs = []
        for h in range(N_CROSS_HEADS):
            sl = slice(h * hd, (h + 1) * hd)
            s = lax.dot_general(qc_ref[:, sl], mk_ref[:, sl].astype(BF16), _NT, preferred_element_type=F32)
            (e,), r = _softmax_parts([s])
            outs.append(jnp.dot((e * r).astype(BF16), mv_ref[:, sl].astype(BF16), preferred_element_type=F32))
        o = jnp.concatenate(outs, axis=1).astype(BF16)
        x2_ref[...] = x1_ref[...] + jnp.dot(o, wo_ref[...], preferred_element_type=F32)

    if len(rest) == 2:
        @pl.when(pl.program_id(0) >= n_own)
        def _():
            x2_ref[...] = rest[0][...]


def _cross_attn(x1, qc, mk, mv, wo, tm, tiles_per_batch, tail=None):
    n, d = x1.shape
    m = mk.shape[1]
    n_own = n // tm
    n_tail = 0 if tail is None else tail.shape[0] // tm
    own = lambda i: (jnp.minimum(i, n_own - 1), 0)
    fix = lambda i: (0, 0)
    mem = pl.BlockSpec((None, m, d), lambda i: (jnp.minimum(i, n_own - 1) // tiles_per_batch, 0, 0))
    in_specs = [pl.BlockSpec((tm, d), own), pl.BlockSpec((tm, d), own), mem, mem, pl.BlockSpec(wo.shape, fix)]
    args = [x1, qc, mk, mv, wo]
    if tail is not None:
        in_specs.append(pl.BlockSpec((tm, d), lambda i: (jnp.maximum(i - n_own, 0), 0)))
        args.append(tail)
    return pl.pallas_call(
        functools.partial(_cross_attn_body, n_own=n_own),
        grid=(n_own + n_tail,),
        in_specs=in_specs,
        out_specs=pl.BlockSpec((tm, d), lambda i: (i, 0)),
        out_shape=_sds((n + n_tail * tm, d), F32),
        compiler_params=_cparams("arbitrary"),
        name="cross_attn",
    )(*args)


def _router_body(x_ref, g_ref, wr_ref, br_ref, hm_ref, d_ref, gt_ref, cnt_ref, brel_ref, tot_ref, cur_ref):
    @pl.when(pl.program_id(0) == 0)
    def _():
        cur_ref[...] = jnp.zeros_like(cur_ref)

    n_exp = wr_ref.shape[0]
    tb = x_ref.shape[0]
    hm = _rms(x_ref[...], g_ref[...])
    hm_ref[...] = hm.astype(BF16)
    logits = lax.dot_general(wr_ref[...], hm, _NT, precision=lax.Precision.HIGHEST,
                             preferred_element_type=F32) + br_ref[...]
    eidx = lax.broadcasted_iota(I32, (n_exp, tb), 0)
    work = logits
    sel = None
    top = None
    for k in range(TOP_K):
        m = jnp.max(work, axis=0, keepdims=True)
        top = m if k == 0 else top
        first = jnp.min(jnp.where(work == m, eidx, n_exp), axis=0, keepdims=True)
        pick = eidx == first
        sel = pick if k == 0 else jnp.logical_or(sel, pick)
        work = jnp.where(pick, -jnp.inf, work)
    e = jnp.where(sel, jnp.exp(logits - top), 0.0)
    gt_ref[...] = e * (1.0 / jnp.sum(e, axis=0, keepdims=True))
    self32 = jnp.where(sel, 1.0, 0.0)
    before = jnp.where(lax.broadcasted_iota(I32, (tb, tb), 0) < lax.broadcasted_iota(I32, (tb, tb), 1), 1.0, 0.0)
    rank = jnp.dot(self32.astype(BF16), before.astype(BF16), preferred_element_type=F32).astype(I32)
    d_ref[...] = jnp.where(sel, rank, NO_SLOT)
    cnt = jnp.broadcast_to(jnp.sum(self32, axis=1, keepdims=True).astype(I32), cnt_ref.shape)
    cnt_ref[...] = cnt
    brel_ref[...] = cur_ref[...]
    cur_ref[...] = cur_ref[...] + cnt
    tot_ref[...] = cur_ref[...]


def _router(x2, g, wr_t, br):
    n, d = x2.shape
    n_exp = wr_t.shape[0]
    tb = ROUTE_BLOCK
    nb = n // tb
    fix = lambda i: (0, 0)
    blk3 = lambda w: pl.BlockSpec((None, n_exp, w), lambda i: (i, 0, 0))
    return pl.pallas_call(
        _router_body,
        grid=(nb,),
        in_specs=[pl.BlockSpec((tb, d), lambda i: (i, 0)), pl.BlockSpec((1, d), fix), pl.BlockSpec((n_exp, d), fix),
                  pl.BlockSpec((n_exp, 1), fix)],
        out_specs=[pl.BlockSpec((tb, d), lambda i: (i, 0)), blk3(tb), blk3(tb), blk3(V7X_LANES), blk3(V7X_LANES),
                   pl.BlockSpec((n_exp, V7X_LANES), fix)],
        out_shape=[_sds((n, d), BF16), _sds((nb, n_exp, tb), I32), _sds((nb, n_exp, tb), F32),
                   _sds((nb, n_exp, V7X_LANES), I32), _sds((nb, n_exp, V7X_LANES), I32), _sds((n_exp, V7X_LANES), I32)],
        scratch_shapes=[pltpu.VMEM((n_exp, V7X_LANES), I32)],
        compiler_params=_cparams("arbitrary"),
        name="moe_router",
    )(x2, g, wr_t, br)


def _region_tiles(total):
    return (total + PIECE_ROWS + FFN_TILE - 1) // FFN_TILE


def _plan_body(tot_ref, start_ref, tile_e_ref, nt_ref, *, n_exp, max_tiles):
    shift = int(math.log2(FFN_TILE))

    def per_expert(e, pos):
        start_ref[e] = pos * FFN_TILE
        nt = lax.shift_right_logical(tot_ref[e] + (PIECE_ROWS + FFN_TILE - 1), shift)

        def mark(j, c):
            tile_e_ref[pos + j] = e
            return c

        lax.fori_loop(0, nt, mark, 0)
        return pos + nt

    n = lax.fori_loop(0, n_exp, per_expert, jnp.int32(0))
    nt_ref[0] = n

    def fill(j, c):
        tile_e_ref[j] = n_exp - 1
        return c

    lax.fori_loop(n, max_tiles, fill, 0)


def _plan(tot, max_tiles):
    n_exp = tot.shape[0]
    smem = pl.BlockSpec(memory_space=pltpu.SMEM)
    return pl.pallas_call(
        functools.partial(_plan_body, n_exp=n_exp, max_tiles=max_tiles),
        in_specs=[smem],
        out_specs=[smem, smem, smem],
        out_shape=[_sds((n_exp,), I32), _sds((max_tiles,), I32), _sds((1,), I32)],
        name="moe_plan",
    )(tot)


def _window(start_ref, brel_ref, cnt_ref, b, e, n_exp):
    base = start_ref[e] + brel_ref[b * n_exp + e]
    return base, jnp.bitwise_and(base, GROUP - 1), cnt_ref[b * n_exp + e]


def _num_rounds(start_ref, brel_ref, cnt_ref, b, n_exp):
    shift = int(math.log2(PIECE_ROWS))

    def f(e, m):
        _, s, n = _window(start_ref, brel_ref, cnt_ref, b, e, n_exp)
        return jnp.maximum(m, lax.shift_right_logical(s + n + (PIECE_ROWS - 1), shift))

    return lax.fori_loop(0, n_exp, f, jnp.int32(1))


def _perm_rows(d_ref, start_ref, brel_ref, cnt_ref, b, r, e, n_exp):
    tb = d_ref.shape[1]
    _, s, _ = _window(start_ref, brel_ref, cnt_ref, b, e, n_exp)
    tgt = d_ref[e:e + 1, :] + (s - PIECE_ROWS * r)
    return lax.broadcasted_iota(I32, (PIECE_ROWS, tb), 0) == tgt


EXPERTS_PER_DOT = 8


def _dispatch_body(brel_ref, cnt_ref, start_ref, tot_ref, hm_ref, d_ref, xs_ref, pieces_ref, carry_ref, zero_ref,
                   pend_ref, sem, *, n_exp):
    b = pl.program_id(0)
    p_rows, grp = PIECE_ROWS, GROUP
    d = hm_ref.shape[1]
    win = functools.partial(_window, start_ref, brel_ref, cnt_ref, b, n_exp=n_exp)

    @pl.when(b == 0)
    def _():
        carry_ref[...] = jnp.zeros_like(carry_ref)
        zero_ref[...] = jnp.zeros_like(zero_ref)

        def n_fill(e):
            region_end = start_ref[e] + _region_tiles(tot_ref[e]) * FFN_TILE
            z0 = start_ref[e] + jnp.bitwise_and(tot_ref[e] + (grp - 1), -grp)
            return z0, lax.shift_right_logical(region_end - z0 + (p_rows - 1), int(math.log2(p_rows)))

        def fill_copy(e, q):
            z0, _ = n_fill(e)
            return pltpu.make_async_copy(zero_ref.at[pl.ds(0, p_rows)],
                                         xs_ref.at[pl.ds(pl.multiple_of(z0 + q * p_rows, grp), p_rows)], sem.at[0])

        def start_e(e, c):
            lax.fori_loop(0, n_fill(e)[1], lambda q, c2: (fill_copy(e, q).start(), c2)[1], 0)
            return c

        def wait_e(e, c):
            lax.fori_loop(0, n_fill(e)[1], lambda q, c2: (fill_copy(e, q).wait(), c2)[1], 0)
            return c

        lax.fori_loop(0, n_exp, start_e, 0)
        lax.fori_loop(0, n_exp, wait_e, 0)

        used = lax.shift_right_logical(start_ref[n_exp - 1], int(math.log2(FFN_TILE))) + _region_tiles(tot_ref[n_exp - 1])
        total_tiles = (xs_ref.shape[0] - p_rows) // FFN_TILE

        def tail_copy(t):
            return pltpu.make_async_copy(zero_ref, xs_ref.at[pl.ds(pl.multiple_of(t * FFN_TILE, FFN_TILE), FFN_TILE)],
                                         sem.at[0])

        slack_copy = pltpu.make_async_copy(zero_ref.at[pl.ds(0, p_rows)],
                                           xs_ref.at[pl.ds(total_tiles * FFN_TILE, p_rows)], sem.at[0])
        lax.fori_loop(used, total_tiles, lambda t, c: (tail_copy(t).start(), c)[1], 0)
        slack_copy.start()
        lax.fori_loop(used, total_tiles, lambda t, c: (tail_copy(t).wait(), c)[1], 0)
        slack_copy.wait()

        pend_ref[0] = 0

    slot = lax.rem(b, 2)
    buf = pieces_ref.at[slot]
    log_p, log_g = int(math.log2(p_rows)), int(math.log2(grp))

    def piece_copy(blk, sl, e, r, sem_ref):
        base, s, _ = _window(start_ref, brel_ref, cnt_ref, blk, e, n_exp)
        dst = pl.multiple_of(base - s + r * p_rows, grp)
        return pltpu.make_async_copy(pieces_ref.at[sl, pl.ds(pl.multiple_of(e * p_rows, p_rows), p_rows)],
                                     xs_ref.at[pl.ds(dst, p_rows)], sem_ref)

    def fill_pieces(r):
        for g0 in range(0, n_exp, EXPERTS_PER_DOT):
            pm = jnp.concatenate(
                [jnp.where(_perm_rows(d_ref, start_ref, brel_ref, cnt_ref, b, r, e, n_exp), 1.0, 0.0).astype(BF16)
                 for e in range(g0, g0 + EXPERTS_PER_DOT)], axis=0)
            buf[g0 * p_rows:(g0 + EXPERTS_PER_DOT) * p_rows, :] = jnp.dot(
                pm, hm_ref[...], preferred_element_type=F32).astype(BF16)

    fill_pieces(0)
    for e in range(n_exp):
        _, s, n = win(e)
        rows = slice(e * p_rows, e * p_rows + grp)
        crow = slice(e * grp, (e + 1) * grp)
        keep = lax.broadcasted_iota(I32, (grp, d), 0) < s
        buf[rows, :] = jnp.where(keep, carry_ref[crow, :], buf[rows, :])
        g = jnp.minimum(lax.shift_right_logical(s + n, log_g), p_rows // grp - 1)
        last = buf[pl.ds(pl.multiple_of(e * p_rows + g * grp, grp), grp), :]
        carry_ref[crow, :] = jnp.where(s + n < p_rows, last, jnp.zeros_like(last))

    @pl.when(pend_ref[0] == 1)
    def _():
        for e in range(n_exp):
            piece_copy(b - 1, 1 - slot, e, 0, sem.at[1]).wait()

    for e in range(n_exp):
        piece_copy(b, slot, e, 0, sem.at[1]).start()
    pend_ref[0] = 1

    n_rounds = functools.reduce(jnp.maximum, [lax.shift_right_logical(win(e)[1] + win(e)[2] + (p_rows - 1), log_p)
                                              for e in range(n_exp)])

    def extra_round(r, c):
        fill_pieces(r)

        def start_e(e, c2):
            _, s, n = win(e)

            @pl.when(lax.shift_right_logical(s + n, log_p) == r)
            def _():
                g = lax.shift_right_logical(jnp.bitwise_and(s + n, p_rows - 1), log_g)
                carry_ref[pl.ds(pl.multiple_of(e * grp, grp), grp), :] = buf[
                    pl.ds(pl.multiple_of(e * p_rows + g * grp, grp), grp), :]

            @pl.when(s + n > r * p_rows)
            def _():
                piece_copy(b, slot, e, r, sem.at[0]).start()

            return c2

        def wait_e(e, c2):
            _, s, n = win(e)

            @pl.when(s + n > r * p_rows)
            def _():
                piece_copy(b, slot, e, r, sem.at[0]).wait()

            return c2

        lax.fori_loop(0, n_exp, start_e, 0)
        lax.fori_loop(0, n_exp, wait_e, 0)
        return c

    def drain():
        for e in range(n_exp):
            piece_copy(b, slot, e, 0, sem.at[1]).wait()
        pend_ref[0] = 0

    @pl.when(n_rounds > 1)
    def _():
        drain()
        lax.fori_loop(1, n_rounds, extra_round, 0)

    @pl.when(jnp.logical_and(b == pl.num_programs(0) - 1, pend_ref[0] == 1))
    def _():
        drain()


def _dispatch(brel, cnt, start, tot, hm, dmat, rows_max):
    n, d = hm.shape
    nb, n_exp, tb = dmat.shape
    grid_spec = pltpu.PrefetchScalarGridSpec(
        num_scalar_prefetch=4,
        grid=(nb,),
        in_specs=[pl.BlockSpec((tb, d), lambda i, *_: (i, 0)), pl.BlockSpec((None, n_exp, tb), lambda i, *_: (i, 0, 0))],
        out_specs=pl.BlockSpec(memory_space=pl.ANY),
        scratch_shapes=[pltpu.VMEM((2, n_exp * PIECE_ROWS, d), BF16), pltpu.VMEM((n_exp * GROUP, d), BF16),
                        pltpu.VMEM((FFN_TILE, d), BF16), pltpu.SMEM((1,), I32), pltpu.SemaphoreType.DMA((2,))],
    )
    return pl.pallas_call(
        functools.partial(_dispatch_body, n_exp=n_exp),
        grid_spec=grid_spec,
        out_shape=_sds((rows_max, d), BF16),
        compiler_params=_cparams("arbitrary"),
        name="moe_dispatch",
    )(brel, cnt, start, tot, hm, dmat)


def _combine_body(brel_ref, cnt_ref, start_ref, x2_ref, d_ref, gt_ref, gf_ref, ys_ref, yp_ref, ysm_ref, pieces_ref,
                  acc_ref, sem, *, n_exp, nb_prompt, final):
    b = pl.program_id(0)
    nb = pl.num_programs(0)
    p_rows, grp = PIECE_ROWS, GROUP
    win = functools.partial(_window, start_ref, brel_ref, cnt_ref, b, n_exp=n_exp)
    slot = lax.rem(b, 2)
    buf = pieces_ref.at[slot]

    def piece_copy(blk, sl, e, r, sem_ref):
        base, s, _ = _window(start_ref, brel_ref, cnt_ref, blk, e, n_exp)
        src = pl.multiple_of(base - s + r * p_rows, grp)
        return pltpu.make_async_copy(ys_ref.at[pl.ds(src, p_rows)],
                                     pieces_ref.at[sl, pl.ds(pl.multiple_of(e * p_rows, p_rows), p_rows)], sem_ref)

    @pl.when(b == 0)
    def _():
        for e in range(n_exp):
            piece_copy(b, slot, e, 0, sem.at[1 + slot]).start()

    @pl.when(b + 1 < nb)
    def _():
        for e in range(n_exp):
            piece_copy(b + 1, 1 - slot, e, 0, sem.at[2 - slot]).start()

    for e in range(n_exp):
        piece_copy(b, slot, e, 0, sem.at[1 + slot]).wait()

    def scatter(r):
        parts = []
        for g0 in range(0, n_exp, EXPERTS_PER_DOT):
            pms, ys = [], []
            for e in range(g0, g0 + EXPERTS_PER_DOT):
                hit = _perm_rows(d_ref, start_ref, brel_ref, cnt_ref, b, r, e, n_exp)
                gate = jnp.sum(jnp.where(hit, gt_ref[e:e + 1, :], 0.0), axis=1, keepdims=True)
                pms.append(jnp.where(hit, 1.0, 0.0).astype(BF16))
                ys.append((buf[e * p_rows:(e + 1) * p_rows, :].astype(F32) * gate).astype(BF16))
            parts.append(lax.dot_general(jnp.concatenate(pms, axis=0), jnp.concatenate(ys, axis=0), _TN,
                                         preferred_element_type=F32))
        return functools.reduce(jnp.add, parts)

    acc_ref[...] = x2_ref[...] + scatter(0)

    def extra_round(r, c):
        def start_e(e, c2):
            _, s, n = win(e)

            @pl.when(s + n > r * p_rows)
            def _():
                piece_copy(b, slot, e, r, sem.at[0]).start()

            return c2

        def wait_e(e, c2):
            _, s, n = win(e)

            @pl.when(s + n > r * p_rows)
            def _():
                piece_copy(b, slot, e, r, sem.at[0]).wait()

            return c2

        lax.fori_loop(0, n_exp, start_e, 0)
        lax.fori_loop(0, n_exp, wait_e, 0)
        acc_ref[...] += scatter(r)
        return c

    lax.fori_loop(1, _num_rounds(start_ref, brel_ref, cnt_ref, b, n_exp), extra_round, 0)
    y = _rms(acc_ref[...], gf_ref[...]) if final else acc_ref[...]

    @pl.when(b < nb_prompt)
    def _():
        yp_ref[...] = y

    @pl.when(b >= nb_prompt)
    def _():
        ysm_ref[...] = y


def _combine(brel, cnt, start, x2, dmat, gates, gf, ys, n_prompt, final):
    n, d = x2.shape
    nb, n_exp, tb = dmat.shape
    nbp = n_prompt // tb
    blk3 = pl.BlockSpec((None, n_exp, tb), lambda i, *_: (i, 0, 0))
    grid_spec = pltpu.PrefetchScalarGridSpec(
        num_scalar_prefetch=3,
        grid=(nb,),
        in_specs=[pl.BlockSpec((tb, d), lambda i, *_: (i, 0)), blk3, blk3, pl.BlockSpec((1, d), lambda i, *_: (0, 0)),
                  pl.BlockSpec(memory_space=pl.ANY)],
        out_specs=[pl.BlockSpec((tb, d), lambda i, *_: (jnp.minimum(i, nbp - 1), 0)),
                   pl.BlockSpec((tb, d), lambda i, *_: (jnp.maximum(i - nbp, 0), 0))],
        scratch_shapes=[pltpu.VMEM((2, n_exp * PIECE_ROWS, d), BF16), pltpu.VMEM((tb, d), F32),
                        pltpu.SemaphoreType.DMA((3,))],
    )
    return pl.pallas_call(
        functools.partial(_combine_body, n_exp=n_exp, nb_prompt=nbp, final=final),
        grid_spec=grid_spec,
        out_shape=[_sds((n_prompt, d), F32), _sds((n - n_prompt, d), F32)],
        compiler_params=_cparams("arbitrary"),
        name="moe_combine",
    )(brel, cnt, start, x2, dmat, gates, gf, ys)


SPLIT = 2 * V7X_LANES


def _ffn_body(tile_e_ref, nt_ref, x_ref, wgu_ref, bg_ref, bu_ref, wd_ref, bd_ref, y_ref, wgu_s, wd_s):
    i = pl.program_id(0)
    live = i < nt_ref[0]
    half = SPLIT // 2

    @pl.when(jnp.logical_not(live))
    def _():
        y_ref[...] = jnp.zeros_like(y_ref)

    @pl.when(jnp.logical_and(live, jnp.logical_or(i == 0, tile_e_ref[i] != tile_e_ref[jnp.maximum(i - 1, 0)])))
    def _():
        r = lax.broadcasted_iota(I32, (SPLIT, SPLIT), 0)
        c = lax.broadcasted_iota(I32, (SPLIT, SPLIT), 1)
        pick = jnp.where(r == jnp.where(c < half, 2 * c, 2 * (c - half) + 1), 1.0, 0.0).astype(BF16)
        for j in range(wgu_ref.shape[1] // SPLIT):
            blk = wgu_ref[:, j * SPLIT:(j + 1) * SPLIT].astype(BF16)
            wgu_s[:, j * SPLIT:(j + 1) * SPLIT] = jnp.dot(blk, pick, preferred_element_type=F32).astype(BF16)
        wd_s[...] = wd_ref[...].astype(BF16)

    @pl.when(live)
    def _():
        x = x_ref[...]
        acts = []
        for j in range(wgu_s.shape[1] // SPLIT):
            gu = jnp.dot(x, wgu_s[:, j * SPLIT:(j + 1) * SPLIT], preferred_element_type=F32)
            gate = jnp.minimum(gu[:, :half] + bg_ref[:, j * half:(j + 1) * half], SWIGLU_LIMIT)
            up = jnp.clip(gu[:, half:] + bu_ref[:, j * half:(j + 1) * half], -SWIGLU_LIMIT, SWIGLU_LIMIT)
            acts.append(((up + 1.0) * gate * jax.nn.sigmoid(SWIGLU_ALPHA * gate)).astype(BF16))
        act = jnp.concatenate(acts, axis=1)
        y_ref[...] = (jnp.dot(act, wd_s[...], preferred_element_type=F32) + bd_ref[...]).astype(y_ref.dtype)


def _ffn(tile_e, nt, xs, wgu, bg, bu, wd, bd):
    d = xs.shape[1]
    f = wd.shape[1]
    max_tiles = tile_e.shape[0]
    tile = lambda i, te, nt: (jnp.minimum(i, nt[0] - 1), 0)
    wsel = lambda i, te, nt: (te[i], 0, 0)
    grid_spec = pltpu.PrefetchScalarGridSpec(
        num_scalar_prefetch=2,
        grid=(max_tiles,),
        in_specs=[pl.BlockSpec((FFN_TILE, d), tile), pl.BlockSpec((None, d, 2 * f), wsel),
                  pl.BlockSpec((None, 1, f), wsel), pl.BlockSpec((None, 1, f), wsel), pl.BlockSpec((None, f, d), wsel),
                  pl.BlockSpec((None, 1, d), wsel)],
        out_specs=pl.BlockSpec((FFN_TILE, d), lambda i, te, nt: (i, 0)),
        scratch_shapes=[pltpu.VMEM((d, 2 * f), BF16), pltpu.VMEM((f, d), BF16)],
    )
    return pl.pallas_call(
        _ffn_body,
        grid_spec=grid_spec,
        out_shape=_sds((max_tiles * FFN_TILE, d), BF16),
        compiler_params=_cparams("arbitrary"),
        name="moe_ffn",
    )(tile_e, nt, xs, wgu, bg, bu, wd, bd)


def _moe(x2, n_prompt, g_moe, wr_t, br, wgu, bg, bu, wd, bd, g_final, final):
    n, d = x2.shape
    n_exp = wr_t.shape[0]
    hm, dmat, gates, cnt, brel, tot = _router(x2, g_moe, wr_t, br)
    cnt = cnt[:, :, 0].reshape(-1)
    brel = brel[:, :, 0].reshape(-1)
    tot = tot[:, 0]
    max_tiles = (TOP_K * n + n_exp * (PIECE_ROWS + FFN_TILE - 1)) // FFN_TILE + 1
    start, tile_e, nt = _plan(tot, max_tiles)
    rows_max = max_tiles * FFN_TILE + PIECE_ROWS
    xs = _dispatch(brel, cnt, start, tot, hm, dmat, rows_max)
    ys = _ffn(tile_e, nt, xs, wgu, bg, bu, wd, bd)
    return _combine(brel, cnt, start, x2, dmat, gates, g_final, ys, n_prompt, final)


def _lambda_init(layer):
    return 0.8 - 0.6 * math.exp(-0.3 * layer)


def kernel(x_prompt, x_sample, cache_k, cache_v, state_conv, cache_mem_k, cache_mem_v, mem_prompt, norm_mix, w_in, lambda_q1, lambda_k1, lambda_q2, lambda_k2, subln_g, conv_w, conv_b, conv_ln_g, conv_ln_b, w_out, norm_cross, norm_mem, w_cq, w_ck, w_cv, w_co, norm_moe, w_router, b_router, w_gu, b_gu, w_down, b_down, norm_final):
    depth = w_in.shape[0]
    bp, sp, d = x_prompt.shape
    bs, ts, _ = x_sample.shape
    n_p, n_s = bp * sp, bs * ts
    n_mem = mem_prompt.shape[1]
    n_exp = w_router.shape[2]
    cc = conv_w.shape[2]
    halo = CONV_WIDTH - 1
    row = lambda a: a.reshape(1, -1)

    xp = x_prompt.reshape(n_p, d)
    xs = x_sample.reshape(n_s, d)
    k_p, v_p, cs_p, mk_p, mv_p, k_s, v_s, cs_s = [], [], [], [], [], [], [], []
    for l in range(depth):
        lam_init = _lambda_init(l)
        lams = [row(lambda_q1[l]), row(lambda_k1[l]), row(lambda_q2[l]), row(lambda_k2[l])]
        sg = row(subln_g[l])
        w_in_b = w_in[l].astype(BF16)
        w_out_b = w_out[l].astype(BF16)
        w_cq_b, w_ck_b, w_cv_b, w_co_b = (w[l].astype(BF16) for w in (w_cq, w_ck, w_cv, w_co))
        conv_args = (conv_w[l], row(conv_b[l]), row(conv_ln_g[l]), row(conv_ln_b[l]))

        k, v, u, qb, kb, vb = _in_proj(xp, row(norm_mix[l]), w_in_b, 512)
        ao = _diff_attn_prompt(lams, sg, qb, kb, vb, bp, sp, lam_init)
        co = _conv_tail(u, jnp.zeros((bp, CONV_HALO, cc), F32), *conv_args, bp, sp)
        x1, qc = _mix_out(xp, ao, co, w_out_b, row(norm_cross[l]), w_cq_b, 512)
        mk, mv, mkb, mvb = _mem_kv(mem_prompt.reshape(bp * n_mem, d), row(norm_mem[l]), w_ck_b, w_cv_b, n_mem)
        tq = min(256, sp)
        x1_p, qc_p, mk_b, mv_b = x1, qc, mkb.reshape(bp, n_mem, d), mvb.reshape(bp, n_mem, d)
        k_p.append(k.reshape(bp, sp, N_DIFF_HEADS, 2 * DIFF_HEAD_DIM))
        v_p.append(v.reshape(bp, sp, N_DIFF_HEADS, 2 * DIFF_HEAD_DIM))
        cs_p.append(u.reshape(bp, sp, cc)[:, sp - halo:])
        mk_p.append(mk.reshape(bp, n_mem, N_CROSS_HEADS, d // N_CROSS_HEADS))
        mv_p.append(mv.reshape(bp, n_mem, N_CROSS_HEADS, d // N_CROSS_HEADS))

        k, v, u, qb, kb, vb = _in_proj(xs, row(norm_mix[l]), w_in_b, min(512, n_s))
        past = cache_k.shape[2]
        ao = _diff_attn_sample(lams, sg, qb, kb, vb, cache_k[l].reshape(bs, past, DIFF_WIDTH),
                               cache_v[l].reshape(bs, past, DIFF_WIDTH), bs, ts, lam_init)
        prefix = jnp.pad(state_conv[l], ((0, 0), (CONV_HALO - halo, 0), (0, 0)))
        co = _conv_tail(u, prefix, *conv_args, bs, ts)
        x1, qc = _mix_out(xs, ao, co, w_out_b, row(norm_cross[l]), w_cq_b, min(512, n_s))
        x2_s = _cross_attn(x1, qc, cache_mem_k[l].reshape(bs, n_mem, d), cache_mem_v[l].reshape(bs, n_mem, d), w_co_b,
                           ts, 1)
        joint = _cross_attn(x1_p, qc_p, mk_b, mv_b, w_co_b, tq, sp // tq, tail=x2_s)
        k_s.append(k.reshape(bs, ts, N_DIFF_HEADS, 2 * DIFF_HEAD_DIM))
        v_s.append(v.reshape(bs, ts, N_DIFF_HEADS, 2 * DIFF_HEAD_DIM))
        u_ext = jnp.concatenate([state_conv[l], u.reshape(bs, ts, cc)], axis=1)
        cs_s.append(u_ext[:, u_ext.shape[1] - halo:])

        bg = b_gu[l][:, 0::2].reshape(n_exp, 1, -1)
        bu = b_gu[l][:, 1::2].reshape(n_exp, 1, -1)
        yp, ysm = _moe(joint, n_p, row(norm_moe[l]), w_router[l].T, b_router[l].reshape(n_exp, 1), w_gu[l], bg, bu,
                       w_down[l], b_down[l].reshape(n_exp, 1, d), row(norm_final), l == depth - 1)
        xp, xs = yp, ysm

    st = lambda xs_: jnp.stack(xs_)
    return (xp.reshape(bp, sp, d), xs.reshape(bs, ts, d), st(k_p), st(v_p), st(cs_p), st(mk_p), st(mv_p), st(k_s),
            st(v_s), st(cs_s))
```

```python
import functools
import math

import jax
import jax.numpy as jnp
from jax import lax
from jax.experimental import pallas as pl
from jax.experimental.pallas import tpu as pltpu

F32 = jnp.float32
BF16 = jnp.bfloat16
I32 = jnp.int32

EPS = 1e-6
CHUNK = 64
N_DIFF_HEADS = 4
DIFF_HEAD_DIM = 64
DIFF_WIDTH = N_DIFF_HEADS * 2 * DIFF_HEAD_DIM
CONV_WIDTH = 31
N_CROSS_HEADS = 4
TOP_K = 4
SWIGLU_LIMIT = 7.0
SWIGLU_ALPHA = 1.702
NEG_INF = -1e30

V7X_LANES = 128
V7X_BF16_ROWS_PER_TILE = 16
V7X_VMEM_LIMIT_BYTES = 56 * 1024 * 1024

ROUTE_BLOCK = 256
PIECE_ROWS = 96
GROUP = V7X_BF16_ROWS_PER_TILE
FFN_TILE = 512
NO_SLOT = -(1 << 20)

_NT = (((1,), (1,)), ((), ()))
_TN = (((0,), (0,)), ((), ()))


def _cparams(*sem):
    return pltpu.CompilerParams(dimension_semantics=sem, vmem_limit_bytes=V7X_VMEM_LIMIT_BYTES)


def _sds(shape, dtype):
    return jax.ShapeDtypeStruct(shape, dtype)


def _rms(x, g):
    return x * lax.rsqrt(jnp.mean(x * x, axis=-1, keepdims=True) + EPS) * g


def _softmax_parts(pieces):
    m = functools.reduce(jnp.maximum, [jnp.max(s, axis=-1, keepdims=True) for s in pieces])
    es = [jnp.exp(s - m) for s in pieces]
    tot = functools.reduce(jnp.add, [jnp.sum(e, axis=-1, keepdims=True) for e in es])
    return es, 1.0 / tot


def _in_proj_body(x_ref, g_ref, w_ref, k_ref, v_ref, u_ref, qb_ref, kb_ref, vb_ref, *, cc):
    h = _rms(x_ref[...], g_ref[...]).astype(BF16)

    def proj(lo, n):
        return jnp.dot(h, w_ref[:, lo:lo + n], preferred_element_type=F32)

    dw = DIFF_WIDTH
    hw = 2 * DIFF_HEAD_DIM
    tm = x_ref.shape[0]

    def store_heads(ref, val):
        for hd in range(N_DIFF_HEADS):
            ref[pl.ds(hd, tm, stride=N_DIFF_HEADS), :] = val[:, hd * hw:(hd + 1) * hw]

    qb_ref[...] = (proj(0, dw) * (DIFF_HEAD_DIM ** -0.5)).astype(BF16)
    k = proj(dw, dw)
    store_heads(k_ref, k)
    kb_ref[...] = k.astype(BF16)
    v = proj(2 * dw, dw)
    store_heads(v_ref, v)
    vb_ref[...] = v.astype(BF16)
    u_ref[...] = proj(3 * dw, cc) * jax.nn.sigmoid(proj(3 * dw + cc, cc))


def _in_proj(x, g, w, tm):
    n, d = x.shape
    cols = w.shape[1]
    cc = (cols - 3 * DIFF_WIDTH) // 2
    hw = 2 * DIFF_HEAD_DIM
    row = lambda i: (i, 0)
    fix = lambda i: (0, 0)
    return pl.pallas_call(
        functools.partial(_in_proj_body, cc=cc),
        grid=(n // tm,),
        in_specs=[pl.BlockSpec((tm, d), row), pl.BlockSpec((1, d), fix), pl.BlockSpec((d, cols), fix)],
        out_specs=[pl.BlockSpec((tm * N_DIFF_HEADS, hw), row), pl.BlockSpec((tm * N_DIFF_HEADS, hw), row),
                   pl.BlockSpec((tm, cc), row),
                   pl.BlockSpec((tm, DIFF_WIDTH), row), pl.BlockSpec((tm, DIFF_WIDTH), row),
                   pl.BlockSpec((tm, DIFF_WIDTH), row)],
        out_shape=[_sds((n * N_DIFF_HEADS, hw), F32), _sds((n * N_DIFF_HEADS, hw), F32), _sds((n, cc), F32),
                   _sds((n, DIFF_WIDTH), BF16), _sds((n, DIFF_WIDTH), BF16), _sds((n, DIFF_WIDTH), BF16)],
        compiler_params=_cparams("arbitrary"),
        name="in_proj",
    )(x, g, w)


def _diff_lambda(lq1, lk1, lq2, lk2, lam_init):
    return (jnp.exp(jnp.sum(lq1[...] * lk1[...], axis=-1, keepdims=True))
            - jnp.exp(jnp.sum(lq2[...] * lk2[...], axis=-1, keepdims=True)) + lam_init)


def _split_components(q):
    lane = lax.broadcasted_iota(I32, q.shape, 1)
    zero = jnp.zeros_like(q)
    return jnp.where(lane < DIFF_HEAD_DIM, q, zero), jnp.where(lane >= DIFF_HEAD_DIM, q, zero)


def _sub_ln(o, sg, lam_init):
    o = o * lax.rsqrt(jnp.mean(o * o, axis=-1, keepdims=True) + EPS)
    return o * sg * (1.0 - lam_init)


def _diff_attn_prompt_body(lq1, lk1, lq2, lk2, sg_ref, q_ref, k_ref, v_ref, o_ref, *, tq, lam_init):
    lam = _diff_lambda(lq1, lk1, lq2, lk2, lam_init)
    seq = q_ref.shape[0]
    vis = (lax.broadcasted_iota(I32, (tq, tq), 1) // CHUNK) <= (lax.broadcasted_iota(I32, (tq, tq), 0) // CHUNK)
    for i in range(seq // tq):
        lo, ext = i * tq, (i + 1) * tq
        es, rs = [], []
        for qc in _split_components(q_ref[lo:ext, :]):
            s_diag = lax.dot_general(qc, k_ref[lo:ext, :], _NT, preferred_element_type=F32)
            pieces = [jnp.where(vis, s_diag, NEG_INF)]
            if i > 0:
                pieces.insert(0, lax.dot_general(qc, k_ref[0:lo, :], _NT, preferred_element_type=F32))
            e, r = _softmax_parts(pieces)
            es.append(e[0] if len(e) == 1 else jnp.concatenate(e, axis=1))
            rs.append(r)
        a = es[0] * rs[0] - es[1] * (lam * rs[1])
        o = jnp.dot(a.astype(BF16), v_ref[0:ext, :], preferred_element_type=F32)
        o_ref[lo:ext, :] = _sub_ln(o, sg_ref[...], lam_init).astype(o_ref.dtype)


def _diff_attn_prompt(lams, sg, qb, kb, vb, batch, seq, lam_init):
    hw = 2 * DIFF_HEAD_DIM
    tq = min(256, seq)
    blk = pl.BlockSpec((seq, hw), lambda b, h: (b, h))
    small = pl.BlockSpec((1, DIFF_HEAD_DIM), lambda b, h: (0, 0))
    return pl.pallas_call(
        functools.partial(_diff_attn_prompt_body, tq=tq, lam_init=lam_init),
        grid=(batch, N_DIFF_HEADS),
        in_specs=[small, small, small, small, pl.BlockSpec((1, hw), lambda b, h: (0, 0)), blk, blk, blk],
        out_specs=blk,
        out_shape=_sds((batch * seq, DIFF_WIDTH), BF16),
        compiler_params=_cparams("arbitrary", "arbitrary"),
        name="diff_attn_prompt",
    )(*lams, sg, qb, kb, vb)


def _diff_attn_sample_body(lq1, lk1, lq2, lk2, sg_ref, q_ref, k_ref, v_ref, ck_ref, cv_ref, o_ref, *, lam_init):
    lam = _diff_lambda(lq1, lk1, lq2, lk2, lam_init)
    t = q_ref.shape[0]
    past = ck_ref.shape[0] // N_DIFF_HEADS
    head_rows = pl.ds(pl.program_id(1), past, stride=N_DIFF_HEADS)
    ck = ck_ref[head_rows, :].astype(BF16)
    q_chunk = (past + lax.broadcasted_iota(I32, (t, 1), 0)) // CHUNK
    vis_past = (lax.broadcasted_iota(I32, (t, past), 1) // CHUNK) <= q_chunk
    vis_new = ((past + lax.broadcasted_iota(I32, (t, t), 1)) // CHUNK) <= q_chunk
    es, rs = [], []
    for qc in _split_components(q_ref[...]):
        s_past = lax.dot_general(qc, ck, _NT, preferred_element_type=F32)
        s_new = lax.dot_general(qc, k_ref[...], _NT, preferred_element_type=F32)
        e, r = _softmax_parts([jnp.where(vis_past, s_past, NEG_INF), jnp.where(vis_new, s_new, NEG_INF)])
        es.append(e)
        rs.append(r)
    a_past = es[0][0] * rs[0] - es[1][0] * (lam * rs[1])
    a_new = es[0][1] * rs[0] - es[1][1] * (lam * rs[1])
    o = (jnp.dot(a_past.astype(BF16), cv_ref[head_rows, :].astype(BF16), preferred_element_type=F32)
         + jnp.dot(a_new.astype(BF16), v_ref[...], preferred_element_type=F32))
    o_ref[...] = _sub_ln(o, sg_ref[...], lam_init).astype(o_ref.dtype)


def _diff_attn_sample(lams, sg, qb, kb, vb, cache_k, cache_v, batch, t, lam_init):
    hw = 2 * DIFF_HEAD_DIM
    rows = cache_k.shape[0] // batch
    blk = pl.BlockSpec((t, hw), lambda b, h: (b, h))
    cblk = pl.BlockSpec((rows, hw), lambda b, h: (b, 0))
    small = pl.BlockSpec((1, DIFF_HEAD_DIM), lambda b, h: (0, 0))
    return pl.pallas_call(
        functools.partial(_diff_attn_sample_body, lam_init=lam_init),
        grid=(batch, N_DIFF_HEADS),
        in_specs=[small, small, small, small, pl.BlockSpec((1, hw), lambda b, h: (0, 0)), blk, blk, blk, cblk, cblk],
        out_specs=blk,
        out_shape=_sds((batch * t, DIFF_WIDTH), BF16),
        compiler_params=_cparams("arbitrary", "arbitrary"),
        name="diff_attn_sample",
    )(*lams, sg, qb, kb, vb, cache_k, cache_v)


CONV_HALO = 32
CONV_ROWS = 32


def _conv_tail_body(u_ref, prev_ref, pre_ref, w_ref, cb_ref, lg_ref, lb_ref, o_ref, slab_ref, conv_ref):
    tc = u_ref.shape[0]
    first = pl.program_id(1) == 0
    off = CONV_HALO - (CONV_WIDTH - 1)
    n_slab = slab_ref.shape[0]
    for sl in range(n_slab):
        lanes = slice(sl * V7X_LANES, (sl + 1) * V7X_LANES)
        slab_ref[sl, 0:CONV_HALO, :] = jnp.where(first, pre_ref[:, lanes], prev_ref[:, lanes])
        slab_ref[sl, CONV_HALO:CONV_HALO + tc, :] = u_ref[:, lanes]

    def one_slab(sl, carry):
        acc = slab_ref[sl, off:off + tc, :] * w_ref[sl, 0:1, :]
        for j in range(1, CONV_WIDTH):
            acc = acc + slab_ref[sl, off + j:off + j + tc, :] * w_ref[sl, j:j + 1, :]
        conv_ref[sl] = acc
        return carry

    lax.fori_loop(0, n_slab, one_slab, 0)
    for r0 in range(0, tc, CONV_ROWS):
        c = jnp.concatenate([conv_ref[sl, r0:r0 + CONV_ROWS, :] for sl in range(n_slab)], axis=1) + cb_ref[...]
        mu = jnp.mean(c, axis=-1, keepdims=True)
        var = jnp.mean(jnp.square(c - mu), axis=-1, keepdims=True)
        cn = (c - mu) * lax.rsqrt(var + EPS) * lg_ref[...] + lb_ref[...]
        o_ref[r0:r0 + CONV_ROWS, :] = (cn * jax.nn.sigmoid(cn)).astype(o_ref.dtype)


def _conv_tail(u, prefix, w, cb, lg, lb, batch, t):
    c = u.shape[1]
    tc = min(256, t)
    nt = t // tc
    hb = tc // CONV_HALO
    n_slab = c // V7X_LANES
    fix = lambda b, i: (0, 0)
    return pl.pallas_call(
        _conv_tail_body,
        grid=(batch, nt),
        in_specs=[pl.BlockSpec((tc, c), lambda b, i: (b * nt + i, 0)),
                  pl.BlockSpec((CONV_HALO, c), lambda b, i: (jnp.maximum((b * nt + i) * hb - 1, 0), 0)),
                  pl.BlockSpec((None, CONV_HALO, c), lambda b, i: (b, 0, 0)),
                  pl.BlockSpec((n_slab, CONV_WIDTH, V7X_LANES), lambda b, i: (0, 0, 0)), pl.BlockSpec((1, c), fix),
                  pl.BlockSpec((1, c), fix), pl.BlockSpec((1, c), fix)],
        out_specs=pl.BlockSpec((tc, c), lambda b, i: (b * nt + i, 0)),
        out_shape=_sds((batch * t, c), BF16),
        scratch_shapes=[pltpu.VMEM((n_slab, CONV_HALO + tc, V7X_LANES), F32), pltpu.VMEM((n_slab, tc, V7X_LANES), F32)],
        compiler_params=_cparams("arbitrary", "arbitrary"),
        name="conv_tail",
    )(u, u, prefix, w.reshape(CONV_WIDTH, n_slab, V7X_LANES).transpose(1, 0, 2), cb, lg, lb)


def _mix_out_body(x_ref, ao_ref, co_ref, wo_ref, g_ref, wq_ref, x1_ref, qc_ref, *, scale):
    dw = ao_ref.shape[1]
    x1 = (x_ref[...] + jnp.dot(ao_ref[...], wo_ref[0:dw, :], preferred_element_type=F32)
          + jnp.dot(co_ref[...], wo_ref[dw:, :], preferred_element_type=F32))
    x1_ref[...] = x1
    h = _rms(x1, g_ref[...]).astype(BF16)
    qc_ref[...] = (jnp.dot(h, wq_ref[...], preferred_element_type=F32) * scale).astype(BF16)


def _mix_out(x, ao, co, wo, g, wq, tm):
    n, d = x.shape
    row = lambda i: (i, 0)
    fix = lambda i: (0, 0)
    scale = (d // N_CROSS_HEADS) ** -0.5
    return pl.pallas_call(
        functools.partial(_mix_out_body, scale=scale),
        grid=(n // tm,),
        in_specs=[pl.BlockSpec((tm, d), row), pl.BlockSpec((tm, ao.shape[1]), row), pl.BlockSpec((tm, co.shape[1]), row),
                  pl.BlockSpec(wo.shape, fix), pl.BlockSpec((1, d), fix), pl.BlockSpec(wq.shape, fix)],
        out_specs=[pl.BlockSpec((tm, d), row), pl.BlockSpec((tm, d), row)],
        out_shape=[_sds((n, d), F32), _sds((n, d), BF16)],
        compiler_params=_cparams("arbitrary"),
        name="mix_out",
    )(x, ao, co, wo, g, wq)


def _mem_kv_body(m_ref, g_ref, wk_ref, wv_ref, mk_ref, mv_ref, mkb_ref, mvb_ref):
    h = _rms(m_ref[...], g_ref[...]).astype(BF16)
    mk = jnp.dot(h, wk_ref[...], preferred_element_type=F32)
    mv = jnp.dot(h, wv_ref[...], preferred_element_type=F32)
    mk_ref[...] = mk
    mv_ref[...] = mv
    mkb_ref[...] = mk.astype(BF16)
    mvb_ref[...] = mv.astype(BF16)


def _mem_kv(mem, g, wk, wv, tm):
    n, d = mem.shape
    row = lambda i: (i, 0)
    fix = lambda i: (0, 0)
    return pl.pallas_call(
        _mem_kv_body,
        grid=(n // tm,),
        in_specs=[pl.BlockSpec((tm, d), row), pl.BlockSpec((1, d), fix), pl.BlockSpec(wk.shape, fix),
                  pl.BlockSpec(wv.shape, fix)],
        out_specs=[pl.BlockSpec((tm, d), row)] * 4,
        out_shape=[_sds((n, d), F32), _sds((n, d), F32), _sds((n, d), BF16), _sds((n, d), BF16)],
        compiler_params=_cparams("arbitrary"),
        name="mem_kv",
    )(mem, g, wk, wv)


def _cross_attn_body(x1_ref, qc_ref, mk_ref, mv_ref, wo_ref, *rest, n_own):
    x2_ref = rest[-1]

    @pl.when(pl.program_id(0) < n_own)
    def _():
        hd = qc_ref.shape[1] // N_CROSS_HEADS
        outs = []
        for h in range(N_CROSS_HEADS):
            sl = slice(h * hd, (h + 1) * hd)
            s = lax.dot_general(qc_ref[:, sl], mk_ref[:, sl].astype(BF16), _NT, preferred_element_type=F32)
            (e,), r = _softmax_parts([s])
            outs.append(jnp.dot((e * r).astype(BF16), mv_ref[:, sl].astype(BF16), preferred_element_type=F32))
        o = jnp.concatenate(outs, axis=1).astype(BF16)
        x2_ref[...] = x1_ref[...] + jnp.dot(o, wo_ref[...], preferred_element_type=F32)

    if len(rest) == 2:
        @pl.when(pl.program_id(0) >= n_own)
        def _():
            x2_ref[...] = rest[0][...]


def _cross_attn(x1, qc, mk, mv, wo, tm, tiles_per_batch, tail=None):
    n, d = x1.shape
    m = mk.shape[1]
    n_own = n // tm
    n_tail = 0 if tail is None else tail.shape[0] // tm
    own = lambda i: (jnp.minimum(i, n_own - 1), 0)
    fix = lambda i: (0, 0)
    mem = pl.BlockSpec((None, m, d), lambda i: (jnp.minimum(i, n_own - 1) // tiles_per_batch, 0, 0))
    in_specs = [pl.BlockSpec((tm, d), own), pl.BlockSpec((tm, d), own), mem, mem, pl.BlockSpec(wo.shape, fix)]
    args = [x1, qc, mk, mv, wo]
    if tail is not None:
        in_specs.append(pl.BlockSpec((tm, d), lambda i: (jnp.maximum(i - n_own, 0), 0)))
        args.append(tail)
    return pl.pallas_call(
        functools.partial(_cross_attn_body, n_own=n_own),
        grid=(n_own + n_tail,),
        in_specs=in_specs,
        out_specs=pl.BlockSpec((tm, d), lambda i: (i, 0)),
        out_shape=_sds((n + n_tail * tm, d), F32),
        compiler_params=_cparams("arbitrary"),
        name="cross_attn",
    )(*args)


def _router_body(x_ref, g_ref, wr_ref, br_ref, hm_ref, d_ref, gt_ref, cnt_ref, brel_ref, tot_ref, cur_ref):
    @pl.when(pl.program_id(0) == 0)
    def _():
        cur_ref[...] = jnp.zeros_like(cur_ref)

    n_exp = wr_ref.shape[0]
    tb = x_ref.shape[0]
    hm = _rms(x_ref[...], g_ref[...])
    hm_hi = hm.astype(BF16)
    hm_ref[...] = hm_hi
    hm_lo = (hm - hm_hi.astype(F32)).astype(BF16)
    wr = wr_ref[...]
    wr_hi = wr.astype(BF16)
    wr_lo = (wr - wr_hi.astype(F32)).astype(BF16)
    nt_dot = lambda a, b: lax.dot_general(a, b, _NT, preferred_element_type=F32)
    logits = nt_dot(wr_hi, hm_hi) + nt_dot(wr_hi, hm_lo) + nt_dot(wr_lo, hm_hi) + br_ref[...]
    eidx = lax.broadcasted_iota(I32, (n_exp, tb), 0)
    work = logits
    sel = None
    top = None
    for k in range(TOP_K):
        m = jnp.max(work, axis=0, keepdims=True)
        top = m if k == 0 else top
        first = jnp.min(jnp.where(work == m, eidx, n_exp), axis=0, keepdims=True)
        pick = eidx == first
        sel = pick if k == 0 else jnp.logical_or(sel, pick)
        work = jnp.where(pick, -jnp.inf, work)
    e = jnp.where(sel, jnp.exp(logits - top), 0.0)
    gt_ref[...] = e * (1.0 / jnp.sum(e, axis=0, keepdims=True))
    self32 = jnp.where(sel, 1.0, 0.0)
    before = jnp.where(lax.broadcasted_iota(I32, (tb, tb), 0) < lax.broadcasted_iota(I32, (tb, tb), 1), 1.0, 0.0)
    rank = jnp.dot(self32.astype(BF16), before.astype(BF16), preferred_element_type=F32).astype(I32)
    d_ref[...] = jnp.where(sel, rank, NO_SLOT)
    cnt = jnp.broadcast_to(jnp.sum(self32, axis=1, keepdims=True).astype(I32), cnt_ref.shape)
    cnt_ref[...] = cnt
    brel_ref[...] = cur_ref[...]
    cur_ref[...] = cur_ref[...] + cnt
    tot_ref[...] = cur_ref[...]


def _router(x2, g, wr_t, br):
    n, d = x2.shape
    n_exp = wr_t.shape[0]
    tb = ROUTE_BLOCK
    nb = n // tb
    fix = lambda i: (0, 0)
    blk3 = lambda w: pl.BlockSpec((None, n_exp, w), lambda i: (i, 0, 0))
    return pl.pallas_call(
        _router_body,
        grid=(nb,),
        in_specs=[pl.BlockSpec((tb, d), lambda i: (i, 0)), pl.BlockSpec((1, d), fix), pl.BlockSpec((n_exp, d), fix),
                  pl.BlockSpec((n_exp, 1), fix)],
        out_specs=[pl.BlockSpec((tb, d), lambda i: (i, 0)), blk3(tb), blk3(tb), blk3(V7X_LANES), blk3(V7X_LANES),
                   pl.BlockSpec((n_exp, V7X_LANES), fix)],
        out_shape=[_sds((n, d), BF16), _sds((nb, n_exp, tb), I32), _sds((nb, n_exp, tb), F32),
                   _sds((nb, n_exp, V7X_LANES), I32), _sds((nb, n_exp, V7X_LANES), I32), _sds((n_exp, V7X_LANES), I32)],
        scratch_shapes=[pltpu.VMEM((n_exp, V7X_LANES), I32)],
        compiler_params=_cparams("arbitrary"),
        name="moe_router",
    )(x2, g, wr_t, br)


def _region_tiles(total):
    return (total + PIECE_ROWS + FFN_TILE - 1) // FFN_TILE


def _plan_body(tot_ref, start_ref, tile_e_ref, nt_ref, *, n_exp, max_tiles):
    shift = int(math.log2(FFN_TILE))

    def per_expert(e, pos):
        start_ref[e] = pos * FFN_TILE
        nt = lax.shift_right_logical(tot_ref[e] + (PIECE_ROWS + FFN_TILE - 1), shift)

        def mark(j, c):
            tile_e_ref[pos + j] = e
            return c

        lax.fori_loop(0, nt, mark, 0)
        return pos + nt

    n = lax.fori_loop(0, n_exp, per_expert, jnp.int32(0))
    nt_ref[0] = n

    def fill(j, c):
        tile_e_ref[j] = n_exp - 1
        return c

    lax.fori_loop(n, max_tiles, fill, 0)


def _plan(tot, max_tiles):
    n_exp = tot.shape[0]
    smem = pl.BlockSpec(memory_space=pltpu.SMEM)
    return pl.pallas_call(
        functools.partial(_plan_body, n_exp=n_exp, max_tiles=max_tiles),
        in_specs=[smem],
        out_specs=[smem, smem, smem],
        out_shape=[_sds((n_exp,), I32), _sds((max_tiles,), I32), _sds((1,), I32)],
        name="moe_plan",
    )(tot)


def _window(start_ref, meta_ref, e, n_exp):
    base = start_ref[e] + meta_ref[0, e]
    return base, jnp.bitwise_and(base, GROUP - 1), meta_ref[0, n_exp + e]


def _num_rounds(start_ref, meta_ref, n_exp):
    ends = [_window(start_ref, meta_ref, e, n_exp) for e in range(n_exp)]
    longest = functools.reduce(jnp.maximum, [s + n for _, s, n in ends])
    return lax.div(longest + (PIECE_ROWS - 1), jnp.int32(PIECE_ROWS))


def _perm_rows(d_ref, start_ref, meta_ref, r, e, n_exp):
    tb = d_ref.shape[1]
    _, s, _ = _window(start_ref, meta_ref, e, n_exp)
    tgt = d_ref[e:e + 1, :] + (s - PIECE_ROWS * r)
    return lax.broadcasted_iota(I32, (PIECE_ROWS, tb), 0) == tgt


EXPERTS_PER_DOT = 8


def _dispatch_body(start_ref, tot_ref, hm_ref, d_ref, meta_ref, xs_ref, pieces_ref, carry_ref, zero_ref,
                   pend_ref, sem, *, n_exp):
    b = pl.program_id(0)
    p_rows, grp = PIECE_ROWS, GROUP
    d = hm_ref.shape[1]
    win = functools.partial(_window, start_ref, meta_ref, n_exp=n_exp)

    @pl.when(b == 0)
    def _():
        carry_ref[...] = jnp.zeros_like(carry_ref)
        zero_ref[...] = jnp.zeros_like(zero_ref)

        def n_fill(e):
            region_end = start_ref[e] + _region_tiles(tot_ref[e]) * FFN_TILE
            z0 = start_ref[e] + jnp.bitwise_and(tot_ref[e] + (grp - 1), -grp)
            return z0, lax.div(region_end - z0 + (p_rows - 1), jnp.int32(p_rows))

        def fill_copy(e, q):
            z0, _ = n_fill(e)
            return pltpu.make_async_copy(zero_ref.at[pl.ds(0, p_rows)],
                                         xs_ref.at[pl.ds(pl.multiple_of(z0 + q * p_rows, grp), p_rows)], sem.at[0])

        def start_e(e, c):
            lax.fori_loop(0, n_fill(e)[1], lambda q, c2: (fill_copy(e, q).start(), c2)[1], 0)
            return c

        def wait_e(e, c):
            lax.fori_loop(0, n_fill(e)[1], lambda q, c2: (fill_copy(e, q).wait(), c2)[1], 0)
            return c

        lax.fori_loop(0, n_exp, start_e, 0)
        lax.fori_loop(0, n_exp, wait_e, 0)

        used = lax.shift_right_logical(start_ref[n_exp - 1], int(math.log2(FFN_TILE))) + _region_tiles(tot_ref[n_exp - 1])
        total_tiles = (xs_ref.shape[0] - p_rows) // FFN_TILE

        def tail_copy(t):
            return pltpu.make_async_copy(zero_ref, xs_ref.at[pl.ds(pl.multiple_of(t * FFN_TILE, FFN_TILE), FFN_TILE)],
                                         sem.at[0])

        slack_copy = pltpu.make_async_copy(zero_ref.at[pl.ds(0, p_rows)],
                                           xs_ref.at[pl.ds(total_tiles * FFN_TILE, p_rows)], sem.at[0])
        lax.fori_loop(used, total_tiles, lambda t, c: (tail_copy(t).start(), c)[1], 0)
        slack_copy.start()
        lax.fori_loop(used, total_tiles, lambda t, c: (tail_copy(t).wait(), c)[1], 0)
        slack_copy.wait()

        pend_ref[0] = 0

    slot = lax.rem(b, 2)
    buf = pieces_ref.at[slot]
    log_g = int(math.log2(grp))

    def piece_copy(sl, e, r, sem_ref):
        base, s, _ = win(e)
        dst = pl.multiple_of(base - s + r * p_rows, grp)
        return pltpu.make_async_copy(pieces_ref.at[sl, pl.ds(pl.multiple_of(e * p_rows, p_rows), p_rows)],
                                     xs_ref.at[pl.ds(dst, p_rows)], sem_ref)

    def fill_pieces(r):
        for g0 in range(0, n_exp, EXPERTS_PER_DOT):
            pm = jnp.concatenate(
                [jnp.where(_perm_rows(d_ref, start_ref, meta_ref, r, e, n_exp), 1.0, 0.0).astype(BF16)
                 for e in range(g0, g0 + EXPERTS_PER_DOT)], axis=0)
            buf[g0 * p_rows:(g0 + EXPERTS_PER_DOT) * p_rows, :] = jnp.dot(
                pm, hm_ref[...], preferred_element_type=F32).astype(BF16)

    fill_pieces(0)
    for e in range(n_exp):
        _, s, n = win(e)
        rows = slice(e * p_rows, e * p_rows + grp)
        crow = slice(e * grp, (e + 1) * grp)
        keep = lax.broadcasted_iota(I32, (grp, d), 0) < s
        buf[rows, :] = jnp.where(keep, carry_ref[crow, :], buf[rows, :])
        g = jnp.minimum(lax.shift_right_logical(s + n, log_g), p_rows // grp - 1)
        last = buf[pl.ds(pl.multiple_of(e * p_rows + g * grp, grp), grp), :]
        carry_ref[crow, :] = jnp.where(s + n < p_rows, last, jnp.zeros_like(last))

    @pl.when(pend_ref[0] == 1)
    def _():
        for e in range(n_exp):
            piece_copy(1 - slot, e, 0, sem.at[1]).wait()

    for e in range(n_exp):
        piece_copy(slot, e, 0, sem.at[1]).start()
    pend_ref[0] = 1

    n_rounds = _num_rounds(start_ref, meta_ref, n_exp)

    def extra_round(r, c):
        fill_pieces(r)

        def start_e(e, c2):
            _, s, n = win(e)

            @pl.when(lax.div(s + n, jnp.int32(p_rows)) == r)
            def _():
                g = lax.shift_right_logical(lax.rem(s + n, jnp.int32(p_rows)), log_g)
                carry_ref[pl.ds(pl.multiple_of(e * grp, grp), grp), :] = buf[
                    pl.ds(pl.multiple_of(e * p_rows + g * grp, grp), grp), :]

            @pl.when(s + n > r * p_rows)
            def _():
                piece_copy(slot, e, r, sem.at[0]).start()

            return c2

        def wait_e(e, c2):
            _, s, n = win(e)

            @pl.when(s + n > r * p_rows)
            def _():
                piece_copy(slot, e, r, sem.at[0]).wait()

            return c2

        lax.fori_loop(0, n_exp, start_e, 0)
        lax.fori_loop(0, n_exp, wait_e, 0)
        return c

    def drain():
        for e in range(n_exp):
            piece_copy(slot, e, 0, sem.at[1]).wait()
        pend_ref[0] = 0

    @pl.when(n_rounds > 1)
    def _():
        drain()
        lax.fori_loop(1, n_rounds, extra_round, 0)

    @pl.when(jnp.logical_and(b == pl.num_programs(0) - 1, pend_ref[0] == 1))
    def _():
        drain()


def _dispatch(start, tot, hm, dmat, meta, rows_max):
    n, d = hm.shape
    nb, n_exp, tb = dmat.shape
    grid_spec = pltpu.PrefetchScalarGridSpec(
        num_scalar_prefetch=2,
        grid=(nb,),
        in_specs=[pl.BlockSpec((tb, d), lambda i, *_: (i, 0)), pl.BlockSpec((None, n_exp, tb), lambda i, *_: (i, 0, 0)),
                  pl.BlockSpec((None, 1, 2 * n_exp), lambda i, *_: (i, 0, 0), memory_space=pltpu.SMEM)],
        out_specs=pl.BlockSpec(memory_space=pl.ANY),
        scratch_shapes=[pltpu.VMEM((2, n_exp * PIECE_ROWS, d), BF16), pltpu.VMEM((n_exp * GROUP, d), BF16),
                        pltpu.VMEM((FFN_TILE, d), BF16), pltpu.SMEM((1,), I32), pltpu.SemaphoreType.DMA((2,))],
    )
    return pl.pallas_call(
        functools.partial(_dispatch_body, n_exp=n_exp),
        grid_spec=grid_spec,
        out_shape=_sds((rows_max, d), BF16),
        compiler_params=_cparams("arbitrary"),
        name="moe_dispatch",
    )(start, tot, hm, dmat, meta)


def _combine_body(start_ref, x2_ref, d_ref, gt_ref, gf_ref, meta_ref, next_ref, ys_ref, yp_ref, ysm_ref, pieces_ref,
                  acc_ref, sem, *, n_exp, nb_prompt, final):
    b = pl.program_id(0)
    nb = pl.num_programs(0)
    p_rows, grp = PIECE_ROWS, GROUP
    win = functools.partial(_window, start_ref, meta_ref, n_exp=n_exp)
    slot = lax.rem(b, 2)
    buf = pieces_ref.at[slot]

    def piece_copy(m_ref, sl, e, r, sem_ref):
        base, s, _ = _window(start_ref, m_ref, e, n_exp)
        src = pl.multiple_of(base - s + r * p_rows, grp)
        return pltpu.make_async_copy(ys_ref.at[pl.ds(src, p_rows)],
                                     pieces_ref.at[sl, pl.ds(pl.multiple_of(e * p_rows, p_rows), p_rows)], sem_ref)

    @pl.when(b == 0)
    def _():
        for e in range(n_exp):
            piece_copy(meta_ref, slot, e, 0, sem.at[1 + slot]).start()

    @pl.when(b + 1 < nb)
    def _():
        for e in range(n_exp):
            piece_copy(next_ref, 1 - slot, e, 0, sem.at[2 - slot]).start()

    for e in range(n_exp):
        piece_copy(meta_ref, slot, e, 0, sem.at[1 + slot]).wait()

    def scatter(r):
        parts = []
        for g0 in range(0, n_exp, EXPERTS_PER_DOT):
            pms, ys = [], []
            for e in range(g0, g0 + EXPERTS_PER_DOT):
                hit = _perm_rows(d_ref, start_ref, meta_ref, r, e, n_exp)
                gate = jnp.sum(jnp.where(hit, gt_ref[e:e + 1, :], 0.0), axis=1, keepdims=True)
                pms.append(jnp.where(hit, 1.0, 0.0).astype(BF16))
                ys.append((buf[e * p_rows:(e + 1) * p_rows, :].astype(F32) * gate).astype(BF16))
            parts.append(lax.dot_general(jnp.concatenate(pms, axis=0), jnp.concatenate(ys, axis=0), _TN,
                                         preferred_element_type=F32))
        return functools.reduce(jnp.add, parts)

    acc_ref[...] = x2_ref[...] + scatter(0)

    def extra_round(r, c):
        def start_e(e, c2):
            _, s, n = win(e)

            @pl.when(s + n > r * p_rows)
            def _():
                piece_copy(meta_ref, slot, e, r, sem.at[0]).start()

            return c2

        def wait_e(e, c2):
            _, s, n = win(e)

            @pl.when(s + n > r * p_rows)
            def _():
                piece_copy(meta_ref, slot, e, r, sem.at[0]).wait()

            return c2

        lax.fori_loop(0, n_exp, start_e, 0)
        lax.fori_loop(0, n_exp, wait_e, 0)
        acc_ref[...] += scatter(r)
        return c

    lax.fori_loop(1, _num_rounds(start_ref, meta_ref, n_exp), extra_round, 0)
    y = _rms(acc_ref[...], gf_ref[...]) if final else acc_ref[...]

    @pl.when(b < nb_prompt)
    def _():
        yp_ref[...] = y

    @pl.when(b >= nb_prompt)
    def _():
        ysm_ref[...] = y


def _combine(start, x2, dmat, gates, gf, meta, ys, n_prompt, final):
    n, d = x2.shape
    nb, n_exp, tb = dmat.shape
    nbp = n_prompt // tb
    blk3 = pl.BlockSpec((None, n_exp, tb), lambda i, *_: (i, 0, 0))
    grid_spec = pltpu.PrefetchScalarGridSpec(
        num_scalar_prefetch=1,
        grid=(nb,),
        in_specs=[pl.BlockSpec((tb, d), lambda i, *_: (i, 0)), blk3, blk3, pl.BlockSpec((1, d), lambda i, *_: (0, 0)),
                  pl.BlockSpec((None, 1, 2 * n_exp), lambda i, *_: (i, 0, 0), memory_space=pltpu.SMEM),
                  pl.BlockSpec((None, 1, 2 * n_exp), lambda i, *_: (jnp.minimum(i + 1, nb - 1), 0, 0),
                               memory_space=pltpu.SMEM),
                  pl.BlockSpec(memory_space=pl.ANY)],
        out_specs=[pl.BlockSpec((tb, d), lambda i, *_: (jnp.minimum(i, nbp - 1), 0)),
                   pl.BlockSpec((tb, d), lambda i, *_: (jnp.maximum(i - nbp, 0), 0))],
        scratch_shapes=[pltpu.VMEM((2, n_exp * PIECE_ROWS, d), BF16), pltpu.VMEM((tb, d), F32),
                        pltpu.SemaphoreType.DMA((3,))],
    )
    return pl.pallas_call(
        functools.partial(_combine_body, n_exp=n_exp, nb_prompt=nbp, final=final),
        grid_spec=grid_spec,
        out_shape=[_sds((n_prompt, d), F32), _sds((n - n_prompt, d), F32)],
        compiler_params=_cparams("arbitrary"),
        name="moe_combine",
    )(start, x2, dmat, gates, gf, meta, meta, ys)


SPLIT = 2 * V7X_LANES


def _ffn_body(tile_e_ref, nt_ref, x_ref, wgu_ref, bg_ref, bu_ref, wd_ref, bd_ref, y_ref, wgu_s, wd_s):
    i = pl.program_id(0)
    live = i < nt_ref[0]
    half = SPLIT // 2

    @pl.when(jnp.logical_not(live))
    def _():
        y_ref[...] = jnp.zeros_like(y_ref)

    @pl.when(jnp.logical_and(live, jnp.logical_or(i == 0, tile_e_ref[i] != tile_e_ref[jnp.maximum(i - 1, 0)])))
    def _():
        r = lax.broadcasted_iota(I32, (SPLIT, SPLIT), 0)
        c = lax.broadcasted_iota(I32, (SPLIT, SPLIT), 1)
        pick = jnp.where(r == jnp.where(c < half, 2 * c, 2 * (c - half) + 1), 1.0, 0.0).astype(BF16)
        for j in range(wgu_ref.shape[1] // SPLIT):
            blk = wgu_ref[:, j * SPLIT:(j + 1) * SPLIT].astype(BF16)
            wgu_s[:, j * SPLIT:(j + 1) * SPLIT] = jnp.dot(blk, pick, preferred_element_type=F32).astype(BF16)
        wd_s[...] = wd_ref[...].astype(BF16)

    @pl.when(live)
    def _():
        x = x_ref[...]
        acts = []
        for j in range(wgu_s.shape[1] // SPLIT):
            gu = jnp.dot(x, wgu_s[:, j * SPLIT:(j + 1) * SPLIT], preferred_element_type=F32)
            gate = jnp.minimum(gu[:, :half] + bg_ref[:, j * half:(j + 1) * half], SWIGLU_LIMIT)
            up = jnp.clip(gu[:, half:] + bu_ref[:, j * half:(j + 1) * half], -SWIGLU_LIMIT, SWIGLU_LIMIT)
            acts.append(((up + 1.0) * gate * jax.nn.sigmoid(SWIGLU_ALPHA * gate)).astype(BF16))
        act = jnp.concatenate(acts, axis=1)
        y_ref[...] = (jnp.dot(act, wd_s[...], preferred_element_type=F32) + bd_ref[...]).astype(y_ref.dtype)


def _ffn(tile_e, nt, xs, wgu, bg, bu, wd, bd):
    d = xs.shape[1]
    f = wd.shape[1]
    max_tiles = tile_e.shape[0]
    tile = lambda i, te, nt: (jnp.minimum(i, nt[0] - 1), 0)
    wsel = lambda i, te, nt: (te[i], 0, 0)
    grid_spec = pltpu.PrefetchScalarGridSpec(
        num_scalar_prefetch=2,
        grid=(max_tiles,),
        in_specs=[pl.BlockSpec((FFN_TILE, d), tile), pl.BlockSpec((None, d, 2 * f), wsel),
                  pl.BlockSpec((None, 1, f), wsel), pl.BlockSpec((None, 1, f), wsel), pl.BlockSpec((None, f, d), wsel),
                  pl.BlockSpec((None, 1, d), wsel)],
        out_specs=pl.BlockSpec((FFN_TILE, d), lambda i, te, nt: (i, 0)),
        scratch_shapes=[pltpu.VMEM((d, 2 * f), BF16), pltpu.VMEM((f, d), BF16)],
    )
    return pl.pallas_call(
        _ffn_body,
        grid_spec=grid_spec,
        out_shape=_sds((max_tiles * FFN_TILE, d), BF16),
        compiler_params=_cparams("arbitrary"),
        name="moe_ffn",
    )(tile_e, nt, xs, wgu, bg, bu, wd, bd)


def _moe(x2, n_prompt, g_moe, wr_t, br, wgu, bg, bu, wd, bd, g_final, final):
    n, d = x2.shape
    n_exp = wr_t.shape[0]
    hm, dmat, gates, cnt, brel, tot = _router(x2, g_moe, wr_t, br)
    meta = jnp.concatenate([brel[:, :, 0], cnt[:, :, 0]], axis=1).reshape(-1, 1, 2 * n_exp)
    tot = tot[:, 0]
    max_tiles = (TOP_K * n + n_exp * (PIECE_ROWS + FFN_TILE - 1)) // FFN_TILE + 1
    start, tile_e, nt = _plan(tot, max_tiles)
    rows_max = max_tiles * FFN_TILE + PIECE_ROWS
    xs = _dispatch(start, tot, hm, dmat, meta, rows_max)
    ys = _ffn(tile_e, nt, xs, wgu, bg, bu, wd, bd)
    return _combine(start, x2, dmat, gates, g_final, meta, ys, n_prompt, final)


def _lambda_init(layer):
    return 0.8 - 0.6 * math.exp(-0.3 * layer)


def kernel(x_prompt, x_sample, cache_k, cache_v, state_conv, cache_mem_k, cache_mem_v, mem_prompt, norm_mix, w_in, lambda_q1, lambda_k1, lambda_q2, lambda_k2, subln_g, conv_w, conv_b, conv_ln_g, conv_ln_b, w_out, norm_cross, norm_mem, w_cq, w_ck, w_cv, w_co, norm_moe, w_router, b_router, w_gu, b_gu, w_down, b_down, norm_final):
    depth = w_in.shape[0]
    bp, sp, d = x_prompt.shape
    bs, ts, _ = x_sample.shape
    n_p, n_s = bp * sp, bs * ts
    n_mem = mem_prompt.shape[1]
    n_exp = w_router.shape[2]
    cc = conv_w.shape[2]
    halo = CONV_WIDTH - 1
    row = lambda a: a.reshape(1, -1)

    assert n_p % ROUTE_BLOCK == 0 and n_s % ROUTE_BLOCK == 0, "token counts must fill whole routing blocks"
    assert sp % 512 == 0 and ts >= halo and ts % 8 == 0, "unsupported sequence lengths"
    xp = x_prompt.reshape(n_p, d)
    xs = x_sample.reshape(n_s, d)
    k_p, v_p, cs_p, mk_p, mv_p, k_s, v_s, cs_s = [], [], [], [], [], [], [], []
    for l in range(depth):
        lam_init = _lambda_init(l)
        lams = [row(lambda_q1[l]), row(lambda_k1[l]), row(lambda_q2[l]), row(lambda_k2[l])]
        sg = row(subln_g[l])
        w_in_b = w_in[l].astype(BF16)
        w_out_b = w_out[l].astype(BF16)
        w_cq_b, w_ck_b, w_cv_b, w_co_b = (w[l].astype(BF16) for w in (w_cq, w_ck, w_cv, w_co))
        conv_args = (conv_w[l], row(conv_b[l]), row(conv_ln_g[l]), row(conv_ln_b[l]))

        k, v, u, qb, kb, vb = _in_proj(xp, row(norm_mix[l]), w_in_b, 512)
        ao = _diff_attn_prompt(lams, sg, qb, kb, vb, bp, sp, lam_init)
        co = _conv_tail(u, jnp.zeros((bp, CONV_HALO, cc), F32), *conv_args, bp, sp)
        x1, qc = _mix_out(xp, ao, co, w_out_b, row(norm_cross[l]), w_cq_b, 512)
        mk, mv, mkb, mvb = _mem_kv(mem_prompt.reshape(bp * n_mem, d), row(norm_mem[l]), w_ck_b, w_cv_b, n_mem)
        tq = math.gcd(512, sp, n_s)
        x1_p, qc_p, mk_b, mv_b = x1, qc, mkb.reshape(bp, n_mem, d), mvb.reshape(bp, n_mem, d)
        k_p.append(k.reshape(bp, sp, N_DIFF_HEADS, 2 * DIFF_HEAD_DIM))
        v_p.append(v.reshape(bp, sp, N_DIFF_HEADS, 2 * DIFF_HEAD_DIM))
        cs_p.append(u.reshape(bp, sp, cc)[:, sp - halo:])
        mk_p.append(mk.reshape(bp, n_mem, N_CROSS_HEADS, d // N_CROSS_HEADS))
        mv_p.append(mv.reshape(bp, n_mem, N_CROSS_HEADS, d // N_CROSS_HEADS))

        k, v, u, qb, kb, vb = _in_proj(xs, row(norm_mix[l]), w_in_b, min(512, n_s))
        ao = _diff_attn_sample(lams, sg, qb, kb, vb, cache_k[l].reshape(-1, 2 * DIFF_HEAD_DIM),
                               cache_v[l].reshape(-1, 2 * DIFF_HEAD_DIM), bs, ts, lam_init)
        prefix = jnp.pad(state_conv[l], ((0, 0), (CONV_HALO - halo, 0), (0, 0)))
        co = _conv_tail(u, prefix, *conv_args, bs, ts)
        x1, qc = _mix_out(xs, ao, co, w_out_b, row(norm_cross[l]), w_cq_b, min(512, n_s))
        x2_s = _cross_attn(x1, qc, cache_mem_k[l].reshape(bs, n_mem, d), cache_mem_v[l].reshape(bs, n_mem, d), w_co_b,
                           ts, 1)
        joint = _cross_attn(x1_p, qc_p, mk_b, mv_b, w_co_b, tq, sp // tq, tail=x2_s)
        k_s.append(k.reshape(bs, ts, N_DIFF_HEADS, 2 * DIFF_HEAD_DIM))
        v_s.append(v.reshape(bs, ts, N_DIFF_HEADS, 2 * DIFF_HEAD_DIM))
        u_ext = jnp.concatenate([state_conv[l], u.reshape(bs, ts, cc)], axis=1)
        cs_s.append(u_ext[:, u_ext.shape[1] - halo:])

        bg = b_gu[l][:, 0::2].reshape(n_exp, 1, -1)
        bu = b_gu[l][:, 1::2].reshape(n_exp, 1, -1)
        yp, ysm = _moe(joint, n_p, row(norm_moe[l]), w_router[l].T, b_router[l].reshape(n_exp, 1), w_gu[l], bg, bu,
                       w_down[l], b_down[l].reshape(n_exp, 1, d), row(norm_final), l == depth - 1)
        xp, xs = yp, ysm

    st = lambda xs_: jnp.stack(xs_)
    return (xp.reshape(bp, sp, d), xs.reshape(bs, ts, d), st(k_p), st(v_p), st(cs_p), st(mk_p), st(mv_p), st(k_s),
            st(v_s), st(cs_s))
```

```python
import functools
import math

import jax
import jax.numpy as jnp
from jax import lax
from jax.experimental import pallas as pl
from jax.experimental.pallas import tpu as pltpu

F32 = jnp.float32
BF16 = jnp.bfloat16
I32 = jnp.int32

EPS = 1e-6
CHUNK = 64
N_DIFF_HEADS = 4
DIFF_HEAD_DIM = 64
DIFF_WIDTH = N_DIFF_HEADS * 2 * DIFF_HEAD_DIM
CONV_WIDTH = 31
N_CROSS_HEADS = 4
TOP_K = 4
SWIGLU_LIMIT = 7.0
SWIGLU_ALPHA = 1.702
NEG_INF = -1e30

V7X_LANES = 128
V7X_BF16_ROWS_PER_TILE = 16
V7X_VMEM_LIMIT_BYTES = 56 * 1024 * 1024

ROUTE_BLOCK = 256
PIECE_ROWS = 96
GROUP = V7X_BF16_ROWS_PER_TILE
FFN_TILE = 512
NO_SLOT = -(1 << 20)

_NT = (((1,), (1,)), ((), ()))
_TN = (((0,), (0,)), ((), ()))


def _cparams(*sem):
    return pltpu.CompilerParams(dimension_semantics=sem, vmem_limit_bytes=V7X_VMEM_LIMIT_BYTES)


def _sds(shape, dtype):
    return jax.ShapeDtypeStruct(shape, dtype)


def _rms(x, g):
    return x * lax.rsqrt(jnp.mean(x * x, axis=-1, keepdims=True) + EPS) * g


def _softmax_parts(pieces):
    m = functools.reduce(jnp.maximum, [jnp.max(s, axis=-1, keepdims=True) for s in pieces])
    es = [jnp.exp(s - m) for s in pieces]
    tot = functools.reduce(jnp.add, [jnp.sum(e, axis=-1, keepdims=True) for e in es])
    return es, 1.0 / tot


def _in_proj_body(x_ref, g_ref, w_ref, k_ref, v_ref, u_ref, qb_ref, kb_ref, vb_ref, *, cc):
    h = _rms(x_ref[...], g_ref[...]).astype(BF16)

    def proj(lo, n):
        return jnp.dot(h, w_ref[:, lo:lo + n], preferred_element_type=F32)

    dw = DIFF_WIDTH
    hw = 2 * DIFF_HEAD_DIM
    tm = x_ref.shape[0]

    def store_heads(ref, val):
        for hd in range(N_DIFF_HEADS):
            ref[pl.ds(hd, tm, stride=N_DIFF_HEADS), :] = val[:, hd * hw:(hd + 1) * hw]

    qb_ref[...] = (proj(0, dw) * (DIFF_HEAD_DIM ** -0.5)).astype(BF16)
    k = proj(dw, dw)
    store_heads(k_ref, k)
    kb_ref[...] = k.astype(BF16)
    v = proj(2 * dw, dw)
    store_heads(v_ref, v)
    vb_ref[...] = v.astype(BF16)
    u_ref[...] = proj(3 * dw, cc) * jax.nn.sigmoid(proj(3 * dw + cc, cc))


def _in_proj(x, g, w, tm):
    n, d = x.shape
    cols = w.shape[1]
    cc = (cols - 3 * DIFF_WIDTH) // 2
    hw = 2 * DIFF_HEAD_DIM
    row = lambda i: (i, 0)
    fix = lambda i: (0, 0)
    return pl.pallas_call(
        functools.partial(_in_proj_body, cc=cc),
        grid=(n // tm,),
        in_specs=[pl.BlockSpec((tm, d), row), pl.BlockSpec((1, d), fix), pl.BlockSpec((d, cols), fix)],
        out_specs=[pl.BlockSpec((tm * N_DIFF_HEADS, hw), row), pl.BlockSpec((tm * N_DIFF_HEADS, hw), row),
                   pl.BlockSpec((tm, cc), row),
                   pl.BlockSpec((tm, DIFF_WIDTH), row), pl.BlockSpec((tm, DIFF_WIDTH), row),
                   pl.BlockSpec((tm, DIFF_WIDTH), row)],
        out_shape=[_sds((n * N_DIFF_HEADS, hw), F32), _sds((n * N_DIFF_HEADS, hw), F32), _sds((n, cc), F32),
                   _sds((n, DIFF_WIDTH), BF16), _sds((n, DIFF_WIDTH), BF16), _sds((n, DIFF_WIDTH), BF16)],
        compiler_params=_cparams("arbitrary"),
        name="in_proj",
    )(x, g, w)


def _diff_lambda(lq1, lk1, lq2, lk2, lam_init):
    return (jnp.exp(jnp.sum(lq1[...] * lk1[...], axis=-1, keepdims=True))
            - jnp.exp(jnp.sum(lq2[...] * lk2[...], axis=-1, keepdims=True)) + lam_init)


def _split_components(q):
    lane = lax.broadcasted_iota(I32, q.shape, 1)
    zero = jnp.zeros_like(q)
    return jnp.where(lane < DIFF_HEAD_DIM, q, zero), jnp.where(lane >= DIFF_HEAD_DIM, q, zero)


def _sub_ln(o, sg, lam_init):
    o = o * lax.rsqrt(jnp.mean(o * o, axis=-1, keepdims=True) + EPS)
    return o * sg * (1.0 - lam_init)


def _diff_attn_prompt_body(lq1, lk1, lq2, lk2, sg_ref, q_ref, k_ref, v_ref, o_ref, *, tq, lam_init):
    lam = _diff_lambda(lq1, lk1, lq2, lk2, lam_init)
    seq = q_ref.shape[0]
    vis = (lax.broadcasted_iota(I32, (tq, tq), 1) // CHUNK) <= (lax.broadcasted_iota(I32, (tq, tq), 0) // CHUNK)
    for i in range(seq // tq):
        lo, ext = i * tq, (i + 1) * tq
        es, rs = [], []
        for qc in _split_components(q_ref[lo:ext, :]):
            s_diag = lax.dot_general(qc, k_ref[lo:ext, :], _NT, preferred_element_type=F32)
            pieces = [jnp.where(vis, s_diag, NEG_INF)]
            if i > 0:
                pieces.insert(0, lax.dot_general(qc, k_ref[0:lo, :], _NT, preferred_element_type=F32))
            e, r = _softmax_parts(pieces)
            es.append(e[0] if len(e) == 1 else jnp.concatenate(e, axis=1))
            rs.append(r)
        a = es[0] * rs[0] - es[1] * (lam * rs[1])
        o = jnp.dot(a.astype(BF16), v_ref[0:ext, :], preferred_element_type=F32)
        o_ref[lo:ext, :] = _sub_ln(o, sg_ref[...], lam_init).astype(o_ref.dtype)


def _diff_attn_prompt(lams, sg, qb, kb, vb, batch, seq, lam_init):
    hw = 2 * DIFF_HEAD_DIM
    tq = min(256, seq)
    blk = pl.BlockSpec((seq, hw), lambda b, h: (b, h))
    small = pl.BlockSpec((1, DIFF_HEAD_DIM), lambda b, h: (0, 0))
    return pl.pallas_call(
        functools.partial(_diff_attn_prompt_body, tq=tq, lam_init=lam_init),
        grid=(batch, N_DIFF_HEADS),
        in_specs=[small, small, small, small, pl.BlockSpec((1, hw), lambda b, h: (0, 0)), blk, blk, blk],
        out_specs=blk,
        out_shape=_sds((batch * seq, DIFF_WIDTH), BF16),
        compiler_params=_cparams("arbitrary", "arbitrary"),
        name="diff_attn_prompt",
    )(*lams, sg, qb, kb, vb)


def _diff_attn_sample_body(lq1, lk1, lq2, lk2, sg_ref, q_ref, k_ref, v_ref, ck_ref, cv_ref, o_ref, *, lam_init):
    lam = _diff_lambda(lq1, lk1, lq2, lk2, lam_init)
    t = q_ref.shape[0]
    past = ck_ref.shape[0] // N_DIFF_HEADS
    head_rows = pl.ds(pl.program_id(1), past, stride=N_DIFF_HEADS)
    ck = ck_ref[head_rows, :].astype(BF16)
    q_chunk = (past + lax.broadcasted_iota(I32, (t, 1), 0)) // CHUNK
    vis_past = (lax.broadcasted_iota(I32, (t, past), 1) // CHUNK) <= q_chunk
    vis_new = ((past + lax.broadcasted_iota(I32, (t, t), 1)) // CHUNK) <= q_chunk
    es, rs = [], []
    for qc in _split_components(q_ref[...]):
        s_past = lax.dot_general(qc, ck, _NT, preferred_element_type=F32)
        s_new = lax.dot_general(qc, k_ref[...], _NT, preferred_element_type=F32)
        e, r = _softmax_parts([jnp.where(vis_past, s_past, NEG_INF), jnp.where(vis_new, s_new, NEG_INF)])
        es.append(e)
        rs.append(r)
    a_past = es[0][0] * rs[0] - es[1][0] * (lam * rs[1])
    a_new = es[0][1] * rs[0] - es[1][1] * (lam * rs[1])
    o = (jnp.dot(a_past.astype(BF16), cv_ref[head_rows, :].astype(BF16), preferred_element_type=F32)
         + jnp.dot(a_new.astype(BF16), v_ref[...], preferred_element_type=F32))
    o_ref[...] = _sub_ln(o, sg_ref[...], lam_init).astype(o_ref.dtype)


def _diff_attn_sample(lams, sg, qb, kb, vb, cache_k, cache_v, batch, t, lam_init):
    hw = 2 * DIFF_HEAD_DIM
    rows = cache_k.shape[0] // batch
    blk = pl.BlockSpec((t, hw), lambda b, h: (b, h))
    cblk = pl.BlockSpec((rows, hw), lambda b, h: (b, 0))
    small = pl.BlockSpec((1, DIFF_HEAD_DIM), lambda b, h: (0, 0))
    return pl.pallas_call(
        functools.partial(_diff_attn_sample_body, lam_init=lam_init),
        grid=(batch, N_DIFF_HEADS),
        in_specs=[small, small, small, small, pl.BlockSpec((1, hw), lambda b, h: (0, 0)), blk, blk, blk, cblk, cblk],
        out_specs=blk,
        out_shape=_sds((batch * t, DIFF_WIDTH), BF16),
        compiler_params=_cparams("arbitrary", "arbitrary"),
        name="diff_attn_sample",
    )(*lams, sg, qb, kb, vb, cache_k, cache_v)


CONV_HALO = 32
CONV_ROWS = 32


def _conv_tail_body(u_ref, prev_ref, pre_ref, w_ref, cb_ref, lg_ref, lb_ref, o_ref, slab_ref, conv_ref):
    tc = u_ref.shape[0]
    first = pl.program_id(1) == 0
    off = CONV_HALO - (CONV_WIDTH - 1)
    n_slab = slab_ref.shape[0]
    for sl in range(n_slab):
        lanes = slice(sl * V7X_LANES, (sl + 1) * V7X_LANES)
        slab_ref[sl, 0:CONV_HALO, :] = jnp.where(first, pre_ref[:, lanes], prev_ref[:, lanes])
        slab_ref[sl, CONV_HALO:CONV_HALO + tc, :] = u_ref[:, lanes]

    def one_slab(sl, carry):
        acc = slab_ref[sl, off:off + tc, :] * w_ref[sl, 0:1, :]
        for j in range(1, CONV_WIDTH):
            acc = acc + slab_ref[sl, off + j:off + j + tc, :] * w_ref[sl, j:j + 1, :]
        conv_ref[sl] = acc
        return carry

    lax.fori_loop(0, n_slab, one_slab, 0)
    for r0 in range(0, tc, CONV_ROWS):
        c = jnp.concatenate([conv_ref[sl, r0:r0 + CONV_ROWS, :] for sl in range(n_slab)], axis=1) + cb_ref[...]
        mu = jnp.mean(c, axis=-1, keepdims=True)
        var = jnp.mean(jnp.square(c - mu), axis=-1, keepdims=True)
        cn = (c - mu) * lax.rsqrt(var + EPS) * lg_ref[...] + lb_ref[...]
        o_ref[r0:r0 + CONV_ROWS, :] = (cn * jax.nn.sigmoid(cn)).astype(o_ref.dtype)


def _conv_tail(u, prefix, w, cb, lg, lb, batch, t):
    c = u.shape[1]
    tc = min(256, t)
    nt = t // tc
    hb = tc // CONV_HALO
    n_slab = c // V7X_LANES
    fix = lambda b, i: (0, 0)
    return pl.pallas_call(
        _conv_tail_body,
        grid=(batch, nt),
        in_specs=[pl.BlockSpec((tc, c), lambda b, i: (b * nt + i, 0)),
                  pl.BlockSpec((CONV_HALO, c), lambda b, i: (jnp.maximum((b * nt + i) * hb - 1, 0), 0)),
                  pl.BlockSpec((None, CONV_HALO, c), lambda b, i: (b, 0, 0)),
                  pl.BlockSpec((n_slab, CONV_WIDTH, V7X_LANES), lambda b, i: (0, 0, 0)), pl.BlockSpec((1, c), fix),
                  pl.BlockSpec((1, c), fix), pl.BlockSpec((1, c), fix)],
        out_specs=pl.BlockSpec((tc, c), lambda b, i: (b * nt + i, 0)),
        out_shape=_sds((batch * t, c), BF16),
        scratch_shapes=[pltpu.VMEM((n_slab, CONV_HALO + tc, V7X_LANES), F32), pltpu.VMEM((n_slab, tc, V7X_LANES), F32)],
        compiler_params=_cparams("arbitrary", "arbitrary"),
        name="conv_tail",
    )(u, u, prefix, w.reshape(CONV_WIDTH, n_slab, V7X_LANES).transpose(1, 0, 2), cb, lg, lb)


def _mix_out_body(x_ref, ao_ref, co_ref, wo_ref, g_ref, wq_ref, x1_ref, qc_ref, *, scale):
    dw = ao_ref.shape[1]
    x1 = (x_ref[...] + jnp.dot(ao_ref[...], wo_ref[0:dw, :], preferred_element_type=F32)
          + jnp.dot(co_ref[...], wo_ref[dw:, :], preferred_element_type=F32))
    x1_ref[...] = x1
    h = _rms(x1, g_ref[...]).astype(BF16)
    qc_ref[...] = (jnp.dot(h, wq_ref[...], preferred_element_type=F32) * scale).astype(BF16)


def _mix_out(x, ao, co, wo, g, wq, tm):
    n, d = x.shape
    row = lambda i: (i, 0)
    fix = lambda i: (0, 0)
    scale = (d // N_CROSS_HEADS) ** -0.5
    return pl.pallas_call(
        functools.partial(_mix_out_body, scale=scale),
        grid=(n // tm,),
        in_specs=[pl.BlockSpec((tm, d), row), pl.BlockSpec((tm, ao.shape[1]), row), pl.BlockSpec((tm, co.shape[1]), row),
                  pl.BlockSpec(wo.shape, fix), pl.BlockSpec((1, d), fix), pl.BlockSpec(wq.shape, fix)],
        out_specs=[pl.BlockSpec((tm, d), row), pl.BlockSpec((tm, d), row)],
        out_shape=[_sds((n, d), F32), _sds((n, d), BF16)],
        compiler_params=_cparams("arbitrary"),
        name="mix_out",
    )(x, ao, co, wo, g, wq)


def _mem_kv_body(m_ref, g_ref, wk_ref, wv_ref, mk_ref, mv_ref, mkb_ref, mvb_ref):
    h = _rms(m_ref[...], g_ref[...]).astype(BF16)
    mk = jnp.dot(h, wk_ref[...], preferred_element_type=F32)
    mv = jnp.dot(h, wv_ref[...], preferred_element_type=F32)
    tm, d = m_ref.shape
    hd = d // N_CROSS_HEADS
    halves = hd // V7X_LANES

    def store_tiled(ref, val):
        for hh in range(N_CROSS_HEADS):
            for c in range(halves):
                col = hh * hd + c * V7X_LANES
                ref[pl.ds(c * N_CROSS_HEADS + hh, tm, stride=halves * N_CROSS_HEADS), :] = val[:, col:col + V7X_LANES]

    store_tiled(mk_ref, mk)
    store_tiled(mv_ref, mv)
    mkb_ref[...] = mk.astype(BF16)
    mvb_ref[...] = mv.astype(BF16)


def _mem_kv(mem, g, wk, wv, tm):
    n, d = mem.shape
    row = lambda i: (i, 0)
    fix = lambda i: (0, 0)
    return pl.pallas_call(
        _mem_kv_body,
        grid=(n // tm,),
        in_specs=[pl.BlockSpec((tm, d), row), pl.BlockSpec((1, d), fix), pl.BlockSpec(wk.shape, fix),
                  pl.BlockSpec(wv.shape, fix)],
        out_specs=[pl.BlockSpec((tm * d // V7X_LANES, V7X_LANES), row)] * 2 + [pl.BlockSpec((tm, d), row)] * 2,
        out_shape=[_sds((n * d // V7X_LANES, V7X_LANES), F32)] * 2 + [_sds((n, d), BF16)] * 2,
        compiler_params=_cparams("arbitrary"),
        name="mem_kv",
    )(mem, g, wk, wv)


def _cross_attn_body(x1_ref, qc_ref, mk_ref, mv_ref, wo_ref, *rest, n_own, tiled):
    x2_ref = rest[-1]

    @pl.when(pl.program_id(0) < n_own)
    def _():
        hd = qc_ref.shape[1] // N_CROSS_HEADS
        halves = hd // V7X_LANES

        def head(ref, h):
            if not tiled:
                return ref[:, h * hd:(h + 1) * hd].astype(BF16)
            m = ref.shape[0] // (halves * N_CROSS_HEADS)
            parts = [ref[pl.ds(c * N_CROSS_HEADS + h, m, stride=halves * N_CROSS_HEADS), :] for c in range(halves)]
            return jnp.concatenate(parts, axis=1).astype(BF16)

        outs = []
        for h in range(N_CROSS_HEADS):
            sl = slice(h * hd, (h + 1) * hd)
            s = lax.dot_general(qc_ref[:, sl], head(mk_ref, h), _NT, preferred_element_type=F32)
            (e,), r = _softmax_parts([s])
            outs.append(jnp.dot((e * r).astype(BF16), head(mv_ref, h), preferred_element_type=F32))
        o = jnp.concatenate(outs, axis=1).astype(BF16)
        x2_ref[...] = x1_ref[...] + jnp.dot(o, wo_ref[...], preferred_element_type=F32)

    if len(rest) == 2:
        @pl.when(pl.program_id(0) >= n_own)
        def _():
            x2_ref[...] = rest[0][...]


def _cross_attn(x1, qc, mk, mv, wo, tm, tiles_per_batch, tail=None):
    n, d = x1.shape
    tiled = mk.shape[2] != d
    n_own = n // tm
    n_tail = 0 if tail is None else tail.shape[0] // tm
    own = lambda i: (jnp.minimum(i, n_own - 1), 0)
    fix = lambda i: (0, 0)
    mem = pl.BlockSpec((None,) + mk.shape[1:], lambda i: (jnp.minimum(i, n_own - 1) // tiles_per_batch, 0, 0))
    in_specs = [pl.BlockSpec((tm, d), own), pl.BlockSpec((tm, d), own), mem, mem, pl.BlockSpec(wo.shape, fix)]
    args = [x1, qc, mk, mv, wo]
    if tail is not None:
        in_specs.append(pl.BlockSpec((tm, d), lambda i: (jnp.maximum(i - n_own, 0), 0)))
        args.append(tail)
    return pl.pallas_call(
        functools.partial(_cross_attn_body, n_own=n_own, tiled=tiled),
        grid=(n_own + n_tail,),
        in_specs=in_specs,
        out_specs=pl.BlockSpec((tm, d), lambda i: (i, 0)),
        out_shape=_sds((n + n_tail * tm, d), F32),
        compiler_params=_cparams("arbitrary"),
        name="cross_attn",
    )(*args)


def _router_body(x_ref, g_ref, wr_ref, br_ref, hm_ref, d_ref, gt_ref, cnt_ref, brel_ref, tot_ref, cur_ref):
    @pl.when(pl.program_id(0) == 0)
    def _():
        cur_ref[...] = jnp.zeros_like(cur_ref)

    n_exp = wr_ref.shape[0]
    tb = x_ref.shape[0]
    hm = _rms(x_ref[...], g_ref[...])
    hm_hi = hm.astype(BF16)
    hm_ref[...] = hm_hi
    hm_lo = (hm - hm_hi.astype(F32)).astype(BF16)
    wr = wr_ref[...]
    wr_hi = wr.astype(BF16)
    wr_lo = (wr - wr_hi.astype(F32)).astype(BF16)
    nt_dot = lambda a, b: lax.dot_general(a, b, _NT, preferred_element_type=F32)
    both = nt_dot(jnp.concatenate([wr_hi, wr_lo], axis=0), hm_hi)
    logits = both[:n_exp] + both[n_exp:] + nt_dot(wr_hi, hm_lo) + br_ref[...]
    eidx = lax.broadcasted_iota(I32, (n_exp, tb), 0)
    work = logits
    sel = None
    top = None
    for k in range(TOP_K):
        m = jnp.max(work, axis=0, keepdims=True)
        top = m if k == 0 else top
        first = jnp.min(jnp.where(work == m, eidx, n_exp), axis=0, keepdims=True)
        pick = eidx == first
        sel = pick if k == 0 else jnp.logical_or(sel, pick)
        work = jnp.where(pick, -jnp.inf, work)
    e = jnp.where(sel, jnp.exp(logits - top), 0.0)
    gt_ref[...] = e * (1.0 / jnp.sum(e, axis=0, keepdims=True))
    self32 = jnp.where(sel, 1.0, 0.0)
    before = jnp.where(lax.broadcasted_iota(I32, (tb, tb), 0) < lax.broadcasted_iota(I32, (tb, tb), 1), 1.0, 0.0)
    rank = jnp.dot(self32.astype(BF16), before.astype(BF16), preferred_element_type=F32).astype(I32)
    d_ref[...] = jnp.where(sel, rank, NO_SLOT)
    cnt = jnp.broadcast_to(jnp.sum(self32, axis=1, keepdims=True).astype(I32), cnt_ref.shape)
    cnt_ref[...] = cnt
    brel_ref[...] = cur_ref[...]
    cur_ref[...] = cur_ref[...] + cnt
    tot_ref[...] = cur_ref[...]


def _router(x2, g, wr_t, br):
    n, d = x2.shape
    n_exp = wr_t.shape[0]
    tb = ROUTE_BLOCK
    nb = n // tb
    fix = lambda i: (0, 0)
    blk3 = lambda w: pl.BlockSpec((None, n_exp, w), lambda i: (i, 0, 0))
    return pl.pallas_call(
        _router_body,
        grid=(nb,),
        in_specs=[pl.BlockSpec((tb, d), lambda i: (i, 0)), pl.BlockSpec((1, d), fix), pl.BlockSpec((n_exp, d), fix),
                  pl.BlockSpec((n_exp, 1), fix)],
        out_specs=[pl.BlockSpec((tb, d), lambda i: (i, 0)), blk3(tb), blk3(tb), blk3(V7X_LANES), blk3(V7X_LANES),
                   pl.BlockSpec((n_exp, V7X_LANES), fix)],
        out_shape=[_sds((n, d), BF16), _sds((nb, n_exp, tb), I32), _sds((nb, n_exp, tb), F32),
                   _sds((nb, n_exp, V7X_LANES), I32), _sds((nb, n_exp, V7X_LANES), I32), _sds((n_exp, V7X_LANES), I32)],
        scratch_shapes=[pltpu.VMEM((n_exp, V7X_LANES), I32)],
        compiler_params=_cparams("arbitrary"),
        name="moe_router",
    )(x2, g, wr_t, br)


def _region_tiles(total):
    return (total + PIECE_ROWS + FFN_TILE - 1) // FFN_TILE


def _plan_body(tot_ref, start_ref, tile_e_ref, nt_ref, *, n_exp, max_tiles):
    shift = int(math.log2(FFN_TILE))

    def per_expert(e, pos):
        start_ref[e] = pos * FFN_TILE
        nt = lax.shift_right_logical(tot_ref[e] + (PIECE_ROWS + FFN_TILE - 1), shift)

        def mark(j, c):
            tile_e_ref[pos + j] = e
            return c

        lax.fori_loop(0, nt, mark, 0)
        return pos + nt

    n = lax.fori_loop(0, n_exp, per_expert, jnp.int32(0))
    nt_ref[0] = n

    def fill(j, c):
        tile_e_ref[j] = n_exp - 1
        return c

    lax.fori_loop(n, max_tiles, fill, 0)


def _plan(tot, max_tiles):
    n_exp = tot.shape[0]
    smem = pl.BlockSpec(memory_space=pltpu.SMEM)
    return pl.pallas_call(
        functools.partial(_plan_body, n_exp=n_exp, max_tiles=max_tiles),
        in_specs=[smem],
        out_specs=[smem, smem, smem],
        out_shape=[_sds((n_exp,), I32), _sds((max_tiles,), I32), _sds((1,), I32)],
        name="moe_plan",
    )(tot)


def _window(start_ref, meta_ref, e, n_exp):
    base = start_ref[e] + meta_ref[0, e]
    return base, jnp.bitwise_and(base, GROUP - 1), meta_ref[0, n_exp + e]


def _num_rounds(start_ref, meta_ref, n_exp):
    ends = [_window(start_ref, meta_ref, e, n_exp) for e in range(n_exp)]
    longest = functools.reduce(jnp.maximum, [s + n for _, s, n in ends])
    return lax.div(longest + (PIECE_ROWS - 1), jnp.int32(PIECE_ROWS))


def _perm_rows(d_ref, start_ref, meta_ref, r, e, n_exp):
    tb = d_ref.shape[1]
    _, s, _ = _window(start_ref, meta_ref, e, n_exp)
    tgt = d_ref[e:e + 1, :] + (s - PIECE_ROWS * r)
    return lax.broadcasted_iota(I32, (PIECE_ROWS, tb), 0) == tgt


EXPERTS_PER_DOT = 8


def _dispatch_body(start_ref, tot_ref, hm_ref, d_ref, meta_ref, xs_ref, pieces_ref, perm_ref, carry_ref, zero_ref,
                   pend_ref, sem, *, n_exp):
    b = pl.program_id(0)
    p_rows, grp = PIECE_ROWS, GROUP
    d = hm_ref.shape[1]
    win = functools.partial(_window, start_ref, meta_ref, n_exp=n_exp)

    @pl.when(b == 0)
    def _():
        carry_ref[...] = jnp.zeros_like(carry_ref)
        zero_ref[...] = jnp.zeros_like(zero_ref)
        pieces_ref[...] = jnp.zeros_like(pieces_ref)
        perm_ref[...] = jnp.zeros_like(perm_ref)

        def n_fill(e):
            region_end = start_ref[e] + _region_tiles(tot_ref[e]) * FFN_TILE
            z0 = start_ref[e] + jnp.bitwise_and(tot_ref[e] + (grp - 1), -grp)
            return z0, lax.div(region_end - z0 + (p_rows - 1), jnp.int32(p_rows))

        def fill_copy(e, q):
            z0, _ = n_fill(e)
            return pltpu.make_async_copy(zero_ref.at[pl.ds(0, p_rows)],
                                         xs_ref.at[pl.ds(pl.multiple_of(z0 + q * p_rows, grp), p_rows)], sem.at[0])

        def start_e(e, c):
            lax.fori_loop(0, n_fill(e)[1], lambda q, c2: (fill_copy(e, q).start(), c2)[1], 0)
            return c

        def wait_e(e, c):
            lax.fori_loop(0, n_fill(e)[1], lambda q, c2: (fill_copy(e, q).wait(), c2)[1], 0)
            return c

        lax.fori_loop(0, n_exp, start_e, 0)
        lax.fori_loop(0, n_exp, wait_e, 0)

        used = lax.shift_right_logical(start_ref[n_exp - 1], int(math.log2(FFN_TILE))) + _region_tiles(tot_ref[n_exp - 1])
        total_tiles = (xs_ref.shape[0] - p_rows) // FFN_TILE

        def tail_copy(t):
            return pltpu.make_async_copy(zero_ref, xs_ref.at[pl.ds(pl.multiple_of(t * FFN_TILE, FFN_TILE), FFN_TILE)],
                                         sem.at[0])

        slack_copy = pltpu.make_async_copy(zero_ref.at[pl.ds(0, p_rows)],
                                           xs_ref.at[pl.ds(total_tiles * FFN_TILE, p_rows)], sem.at[0])
        lax.fori_loop(used, total_tiles, lambda t, c: (tail_copy(t).start(), c)[1], 0)
        slack_copy.start()
        lax.fori_loop(used, total_tiles, lambda t, c: (tail_copy(t).wait(), c)[1], 0)
        slack_copy.wait()

        pend_ref[0] = 0

    slot = lax.rem(b, 2)
    buf = pieces_ref.at[slot]
    log_g = int(math.log2(grp))

    def piece_copy(sl, e, r, sem_ref):
        base, s, _ = win(e)
        dst = pl.multiple_of(base - s + r * p_rows, grp)
        return pltpu.make_async_copy(pieces_ref.at[sl, pl.ds(pl.multiple_of(e * p_rows, p_rows), p_rows)],
                                     xs_ref.at[pl.ds(dst, p_rows)], sem_ref)

    def fill_pieces(r):
        for g0 in range(0, n_exp, EXPERTS_PER_DOT):
            pm = jnp.concatenate(
                [jnp.where(_perm_rows(d_ref, start_ref, meta_ref, r, e, n_exp), 1.0, 0.0).astype(BF16)
                 for e in range(g0, g0 + EXPERTS_PER_DOT)], axis=0)
            buf[g0 * p_rows:(g0 + EXPERTS_PER_DOT) * p_rows, :] = jnp.dot(
                pm, hm_ref[...], preferred_element_type=F32).astype(BF16)

    def merge_carry(e, first_row):
        _, s, n = win(e)
        rows = pl.ds(first_row, grp)
        crow = slice(e * grp, (e + 1) * grp)
        keep = lax.broadcasted_iota(I32, (grp, d), 0) < s
        buf[rows, :] = jnp.where(keep, carry_ref[crow, :], buf[rows, :])
        return crow, s + n

    half = p_rows // 2

    def half_copy(sl, src_row, dst_row):
        aligned = lambda r: r if isinstance(r, int) else pl.multiple_of(r, grp)
        return pltpu.make_async_copy(pieces_ref.at[sl, pl.ds(aligned(src_row), half)],
                                     xs_ref.at[pl.ds(aligned(dst_row), half)], sem.at[1])

    def window_copies(e, src_row, always_both):
        base, s, n = win(e)
        dst = base - s
        half_copy(slot, src_row, dst).start()
        if always_both:
            half_copy(slot, src_row + half, dst + half).start()
            return 2
        both = s + n > half

        @pl.when(both)
        def _():
            half_copy(slot, src_row + half, dst + half).start()

        return 1 + both.astype(I32)

    def wait_previous():
        lax.fori_loop(0, pend_ref[0], lambda i, c: (half_copy(0, 0, 0).wait(), c)[1], 0)

    n_rounds = _num_rounds(start_ref, meta_ref, n_exp)

    @pl.when(n_rounds == 1)
    def _():
        firsts = []
        nxt = jnp.int32(0)
        for e in range(n_exp):
            _, s, n = win(e)
            firsts.append(pl.multiple_of(nxt, grp))
            nxt = nxt + jnp.bitwise_and(s + n + (grp - 1), -grp)
        for e in range(n_exp):
            perm_ref[pl.ds(firsts[e], p_rows), :] = jnp.where(
                _perm_rows(d_ref, start_ref, meta_ref, 0, e, n_exp), 1.0, 0.0).astype(BF16)
        step = EXPERTS_PER_DOT * PIECE_ROWS
        packed_rows = perm_ref.shape[0] - p_rows
        for r0 in range(0, packed_rows, step):
            r1 = min(r0 + step, packed_rows)
            buf[r0:r1, :] = jnp.dot(perm_ref[r0:r1, :], hm_ref[...], preferred_element_type=F32).astype(BF16)
        for e in range(n_exp):
            crow, end = merge_carry(e, firsts[e])
            last = buf[pl.ds(pl.multiple_of(firsts[e] + jnp.bitwise_and(end, -grp), grp), grp), :]
            carry_ref[crow, :] = jnp.where(jnp.bitwise_and(end, grp - 1) != 0, last, jnp.zeros_like(last))
        wait_previous()
        pend_ref[0] = functools.reduce(jnp.add, [window_copies(e, firsts[e], False) for e in range(n_exp)])

    def extra_round(r, c):
        fill_pieces(r)

        def start_e(e, c2):
            _, s, n = win(e)

            @pl.when(lax.div(s + n, jnp.int32(p_rows)) == r)
            def _():
                g = lax.shift_right_logical(lax.rem(s + n, jnp.int32(p_rows)), log_g)
                carry_ref[pl.ds(pl.multiple_of(e * grp, grp), grp), :] = buf[
                    pl.ds(pl.multiple_of(e * p_rows + g * grp, grp), grp), :]

            @pl.when(s + n > r * p_rows)
            def _():
                piece_copy(slot, e, r, sem.at[0]).start()

            return c2

        def wait_e(e, c2):
            _, s, n = win(e)

            @pl.when(s + n > r * p_rows)
            def _():
                piece_copy(slot, e, r, sem.at[0]).wait()

            return c2

        lax.fori_loop(0, n_exp, start_e, 0)
        lax.fori_loop(0, n_exp, wait_e, 0)
        return c

    def drain():
        wait_previous()
        pend_ref[0] = 0

    @pl.when(n_rounds > 1)
    def _():
        fill_pieces(0)
        for e in range(n_exp):
            crow, end = merge_carry(e, e * p_rows)
            g = jnp.minimum(lax.shift_right_logical(end, log_g), p_rows // grp - 1)
            last = buf[pl.ds(pl.multiple_of(e * p_rows + g * grp, grp), grp), :]
            carry_ref[crow, :] = jnp.where(end < p_rows, last, jnp.zeros_like(last))
        wait_previous()
        pend_ref[0] = sum(window_copies(e, e * p_rows, True) for e in range(n_exp))
        drain()
        lax.fori_loop(1, n_rounds, extra_round, 0)

    @pl.when(b == pl.num_programs(0) - 1)
    def _():
        drain()


def _dispatch(start, tot, hm, dmat, meta, rows_max):
    n, d = hm.shape
    nb, n_exp, tb = dmat.shape
    packed_rows = -(-(TOP_K * tb + 2 * (GROUP - 1) * n_exp) // 256) * 256
    assert packed_rows + PIECE_ROWS <= n_exp * PIECE_ROWS
    grid_spec = pltpu.PrefetchScalarGridSpec(
        num_scalar_prefetch=2,
        grid=(nb,),
        in_specs=[pl.BlockSpec((tb, d), lambda i, *_: (i, 0)), pl.BlockSpec((None, n_exp, tb), lambda i, *_: (i, 0, 0)),
                  pl.BlockSpec((None, 1, 2 * n_exp), lambda i, *_: (i, 0, 0), memory_space=pltpu.SMEM)],
        out_specs=pl.BlockSpec(memory_space=pl.ANY),
        scratch_shapes=[pltpu.VMEM((2, n_exp * PIECE_ROWS, d), BF16), pltpu.VMEM((packed_rows + PIECE_ROWS, tb), BF16),
                        pltpu.VMEM((n_exp * GROUP, d), BF16), pltpu.VMEM((FFN_TILE, d), BF16), pltpu.SMEM((1,), I32),
                        pltpu.SemaphoreType.DMA((2,))],
    )
    return pl.pallas_call(
        functools.partial(_dispatch_body, n_exp=n_exp),
        grid_spec=grid_spec,
        out_shape=_sds((rows_max, d), BF16),
        compiler_params=_cparams("arbitrary"),
        name="moe_dispatch",
    )(start, tot, hm, dmat, meta)


def _combine_body(start_ref, x2_ref, d_ref, gt_ref, gf_ref, meta_ref, next_ref, ys_ref, yp_ref, ysm_ref, pieces_ref,
                  acc_ref, sem, *, n_exp, nb_prompt, final):
    b = pl.program_id(0)
    nb = pl.num_programs(0)
    p_rows, grp = PIECE_ROWS, GROUP
    win = functools.partial(_window, start_ref, meta_ref, n_exp=n_exp)
    slot = lax.rem(b, 2)
    buf = pieces_ref.at[slot]

    def piece_copy(m_ref, sl, e, r, sem_ref):
        base, s, _ = _window(start_ref, m_ref, e, n_exp)
        src = pl.multiple_of(base - s + r * p_rows, grp)
        return pltpu.make_async_copy(ys_ref.at[pl.ds(src, p_rows)],
                                     pieces_ref.at[sl, pl.ds(pl.multiple_of(e * p_rows, p_rows), p_rows)], sem_ref)

    @pl.when(b == 0)
    def _():
        for e in range(n_exp):
            piece_copy(meta_ref, slot, e, 0, sem.at[1 + slot]).start()

    @pl.when(b + 1 < nb)
    def _():
        for e in range(n_exp):
            piece_copy(next_ref, 1 - slot, e, 0, sem.at[2 - slot]).start()

    for e in range(n_exp):
        piece_copy(meta_ref, slot, e, 0, sem.at[1 + slot]).wait()

    def scatter(r):
        parts = []
        for g0 in range(0, n_exp, EXPERTS_PER_DOT):
            pms, ys = [], []
            for e in range(g0, g0 + EXPERTS_PER_DOT):
                hit = _perm_rows(d_ref, start_ref, meta_ref, r, e, n_exp)
                gate = jnp.sum(jnp.where(hit, gt_ref[e:e + 1, :], 0.0), axis=1, keepdims=True)
                pms.append(jnp.where(hit, 1.0, 0.0).astype(BF16))
                ys.append((buf[e * p_rows:(e + 1) * p_rows, :].astype(F32) * gate).astype(BF16))
            parts.append(lax.dot_general(jnp.concatenate(pms, axis=0), jnp.concatenate(ys, axis=0), _TN,
                                         preferred_element_type=F32))
        return functools.reduce(jnp.add, parts)

    acc_ref[...] = x2_ref[...] + scatter(0)

    def extra_round(r, c):
        def start_e(e, c2):
            _, s, n = win(e)

            @pl.when(s + n > r * p_rows)
            def _():
                piece_copy(meta_ref, slot, e, r, sem.at[0]).start()

            return c2

        def wait_e(e, c2):
            _, s, n = win(e)

            @pl.when(s + n > r * p_rows)
            def _():
                piece_copy(meta_ref, slot, e, r, sem.at[0]).wait()

            return c2

        lax.fori_loop(0, n_exp, start_e, 0)
        lax.fori_loop(0, n_exp, wait_e, 0)
        acc_ref[...] += scatter(r)
        return c

    lax.fori_loop(1, _num_rounds(start_ref, meta_ref, n_exp), extra_round, 0)
    y = _rms(acc_ref[...], gf_ref[...]) if final else acc_ref[...]

    @pl.when(b < nb_prompt)
    def _():
        yp_ref[...] = y

    @pl.when(b >= nb_prompt)
    def _():
        ysm_ref[...] = y


def _combine(start, x2, dmat, gates, gf, meta, ys, n_prompt, final):
    n, d = x2.shape
    nb, n_exp, tb = dmat.shape
    nbp = n_prompt // tb
    blk3 = pl.BlockSpec((None, n_exp, tb), lambda i, *_: (i, 0, 0))
    grid_spec = pltpu.PrefetchScalarGridSpec(
        num_scalar_prefetch=1,
        grid=(nb,),
        in_specs=[pl.BlockSpec((tb, d), lambda i, *_: (i, 0)), blk3, blk3, pl.BlockSpec((1, d), lambda i, *_: (0, 0)),
                  pl.BlockSpec((None, 1, 2 * n_exp), lambda i, *_: (i, 0, 0), memory_space=pltpu.SMEM),
                  pl.BlockSpec((None, 1, 2 * n_exp), lambda i, *_: (jnp.minimum(i + 1, nb - 1), 0, 0),
                               memory_space=pltpu.SMEM),
                  pl.BlockSpec(memory_space=pl.ANY)],
        out_specs=[pl.BlockSpec((tb, d), lambda i, *_: (jnp.minimum(i, nbp - 1), 0)),
                   pl.BlockSpec((tb, d), lambda i, *_: (jnp.maximum(i - nbp, 0), 0))],
        scratch_shapes=[pltpu.VMEM((2, n_exp * PIECE_ROWS, d), BF16), pltpu.VMEM((tb, d), F32),
                        pltpu.SemaphoreType.DMA((3,))],
    )
    return pl.pallas_call(
        functools.partial(_combine_body, n_exp=n_exp, nb_prompt=nbp, final=final),
        grid_spec=grid_spec,
        out_shape=[_sds((n_prompt, d), F32), _sds((n - n_prompt, d), F32)],
        compiler_params=_cparams("arbitrary"),
        name="moe_combine",
    )(start, x2, dmat, gates, gf, meta, meta, ys)


SPLIT = 2 * V7X_LANES


def _ffn_body(tile_e_ref, nt_ref, x_ref, wgu_ref, bg_ref, bu_ref, wd_ref, bd_ref, y_ref, wgu_s, wd_s):
    i = pl.program_id(0)
    live = i < nt_ref[0]
    half = SPLIT // 2

    @pl.when(jnp.logical_not(live))
    def _():
        y_ref[...] = jnp.zeros_like(y_ref)

    @pl.when(jnp.logical_and(live, jnp.logical_or(i == 0, tile_e_ref[i] != tile_e_ref[jnp.maximum(i - 1, 0)])))
    def _():
        r = lax.broadcasted_iota(I32, (SPLIT, SPLIT), 0)
        c = lax.broadcasted_iota(I32, (SPLIT, SPLIT), 1)
        pick = jnp.where(r == jnp.where(c < half, 2 * c, 2 * (c - half) + 1), 1.0, 0.0).astype(BF16)
        for j in range(wgu_ref.shape[1] // SPLIT):
            blk = wgu_ref[:, j * SPLIT:(j + 1) * SPLIT].astype(BF16)
            wgu_s[:, j * SPLIT:(j + 1) * SPLIT] = jnp.dot(blk, pick, preferred_element_type=F32).astype(BF16)
        wd_s[...] = wd_ref[...].astype(BF16)

    @pl.when(live)
    def _():
        x = x_ref[...]
        acts = []
        for j in range(wgu_s.shape[1] // SPLIT):
            gu = jnp.dot(x, wgu_s[:, j * SPLIT:(j + 1) * SPLIT], preferred_element_type=F32)
            gate = jnp.minimum(gu[:, :half] + bg_ref[:, j * half:(j + 1) * half], SWIGLU_LIMIT)
            up = jnp.clip(gu[:, half:] + bu_ref[:, j * half:(j + 1) * half], -SWIGLU_LIMIT, SWIGLU_LIMIT)
            acts.append(((up + 1.0) * gate * jax.nn.sigmoid(SWIGLU_ALPHA * gate)).astype(BF16))
        act = jnp.concatenate(acts, axis=1)
        y_ref[...] = (jnp.dot(act, wd_s[...], preferred_element_type=F32) + bd_ref[...]).astype(y_ref.dtype)


def _ffn(tile_e, nt, xs, wgu, bg, bu, wd, bd):
    d = xs.shape[1]
    f = wd.shape[1]
    max_tiles = tile_e.shape[0]
    tile = lambda i, te, nt: (jnp.minimum(i, nt[0] - 1), 0)
    wsel = lambda i, te, nt: (te[i], 0, 0)
    grid_spec = pltpu.PrefetchScalarGridSpec(
        num_scalar_prefetch=2,
        grid=(max_tiles,),
        in_specs=[pl.BlockSpec((FFN_TILE, d), tile), pl.BlockSpec((None, d, 2 * f), wsel),
                  pl.BlockSpec((None, 1, f), wsel), pl.BlockSpec((None, 1, f), wsel), pl.BlockSpec((None, f, d), wsel),
                  pl.BlockSpec((None, 1, d), wsel)],
        out_specs=pl.BlockSpec((FFN_TILE, d), lambda i, te, nt: (i, 0)),
        scratch_shapes=[pltpu.VMEM((d, 2 * f), BF16), pltpu.VMEM((f, d), BF16)],
    )
    return pl.pallas_call(
        _ffn_body,
        grid_spec=grid_spec,
        out_shape=_sds((max_tiles * FFN_TILE, d), BF16),
        compiler_params=_cparams("arbitrary"),
        name="moe_ffn",
    )(tile_e, nt, xs, wgu, bg, bu, wd, bd)


def _moe(x2, n_prompt, g_moe, wr_t, br, wgu, bg, bu, wd, bd, g_final, final):
    n, d = x2.shape
    n_exp = wr_t.shape[0]
    hm, dmat, gates, cnt, brel, tot = _router(x2, g_moe, wr_t, br)
    meta = jnp.concatenate([brel[:, :, 0], cnt[:, :, 0]], axis=1).reshape(-1, 1, 2 * n_exp)
    tot = tot[:, 0]
    max_tiles = (TOP_K * n + n_exp * (PIECE_ROWS + FFN_TILE - 1)) // FFN_TILE + 1
    start, tile_e, nt = _plan(tot, max_tiles)
    rows_max = max_tiles * FFN_TILE + PIECE_ROWS
    xs = _dispatch(start, tot, hm, dmat, meta, rows_max)
    ys = _ffn(tile_e, nt, xs, wgu, bg, bu, wd, bd)
    return _combine(start, x2, dmat, gates, g_final, meta, ys, n_prompt, final)


def _lambda_init(layer):
    return 0.8 - 0.6 * math.exp(-0.3 * layer)


def kernel(x_prompt, x_sample, cache_k, cache_v, state_conv, cache_mem_k, cache_mem_v, mem_prompt, norm_mix, w_in, lambda_q1, lambda_k1, lambda_q2, lambda_k2, subln_g, conv_w, conv_b, conv_ln_g, conv_ln_b, w_out, norm_cross, norm_mem, w_cq, w_ck, w_cv, w_co, norm_moe, w_router, b_router, w_gu, b_gu, w_down, b_down, norm_final):
    depth = w_in.shape[0]
    bp, sp, d = x_prompt.shape
    bs, ts, _ = x_sample.shape
    n_p, n_s = bp * sp, bs * ts
    n_mem = mem_prompt.shape[1]
    n_exp = w_router.shape[2]
    cc = conv_w.shape[2]
    halo = CONV_WIDTH - 1
    row = lambda a: a.reshape(1, -1)

    assert n_p % ROUTE_BLOCK == 0 and n_s % ROUTE_BLOCK == 0, "token counts must fill whole routing blocks"
    assert sp % 512 == 0 and ts >= halo and ts % 8 == 0, "unsupported sequence lengths"
    xp = x_prompt.reshape(n_p, d)
    xs = x_sample.reshape(n_s, d)
    k_p, v_p, cs_p, mk_p, mv_p, k_s, v_s, cs_s = [], [], [], [], [], [], [], []
    for l in range(depth):
        lam_init = _lambda_init(l)
        lams = [row(lambda_q1[l]), row(lambda_k1[l]), row(lambda_q2[l]), row(lambda_k2[l])]
        sg = row(subln_g[l])
        w_in_b = w_in[l].astype(BF16)
        w_out_b = w_out[l].astype(BF16)
        w_cq_b, w_ck_b, w_cv_b, w_co_b = (w[l].astype(BF16) for w in (w_cq, w_ck, w_cv, w_co))
        conv_args = (conv_w[l], row(conv_b[l]), row(conv_ln_g[l]), row(conv_ln_b[l]))

        k, v, u, qb, kb, vb = _in_proj(xp, row(norm_mix[l]), w_in_b, 512)
        ao = _diff_attn_prompt(lams, sg, qb, kb, vb, bp, sp, lam_init)
        co = _conv_tail(u, jnp.zeros((bp, CONV_HALO, cc), F32), *conv_args, bp, sp)
        x1, qc = _mix_out(xp, ao, co, w_out_b, row(norm_cross[l]), w_cq_b, 512)
        mk, mv, mkb, mvb = _mem_kv(mem_prompt.reshape(bp * n_mem, d), row(norm_mem[l]), w_ck_b, w_cv_b, n_mem)
        tq = math.gcd(512, sp, n_s)
        x1_p, qc_p, mk_b, mv_b = x1, qc, mkb.reshape(bp, n_mem, d), mvb.reshape(bp, n_mem, d)
        k_p.append(k.reshape(bp, sp, N_DIFF_HEADS, 2 * DIFF_HEAD_DIM))
        v_p.append(v.reshape(bp, sp, N_DIFF_HEADS, 2 * DIFF_HEAD_DIM))
        cs_p.append(u.reshape(bp, sp, cc)[:, sp - halo:])
        hd_c = d // N_CROSS_HEADS
        untile = lambda a: (a.reshape(bp, n_mem, hd_c // V7X_LANES, N_CROSS_HEADS, V7X_LANES)
                            .transpose(0, 1, 3, 2, 4).reshape(bp, n_mem, N_CROSS_HEADS, hd_c))
        mk_p.append(untile(mk))
        mv_p.append(untile(mv))

        k, v, u, qb, kb, vb = _in_proj(xs, row(norm_mix[l]), w_in_b, min(512, n_s))
        ao = _diff_attn_sample(lams, sg, qb, kb, vb, cache_k[l].reshape(-1, 2 * DIFF_HEAD_DIM),
                               cache_v[l].reshape(-1, 2 * DIFF_HEAD_DIM), bs, ts, lam_init)
        prefix = jnp.pad(state_conv[l], ((0, 0), (CONV_HALO - halo, 0), (0, 0)))
        co = _conv_tail(u, prefix, *conv_args, bs, ts)
        x1, qc = _mix_out(xs, ao, co, w_out_b, row(norm_cross[l]), w_cq_b, min(512, n_s))
        hd_c = d // N_CROSS_HEADS
        tile = lambda a: (a.reshape(bs, n_mem, N_CROSS_HEADS, hd_c // V7X_LANES, V7X_LANES)
                          .transpose(0, 1, 3, 2, 4).reshape(bs, n_mem * d // V7X_LANES, V7X_LANES))
        x2_s = _cross_attn(x1, qc, tile(cache_mem_k[l]), tile(cache_mem_v[l]), w_co_b, ts, 1)
        joint = _cross_attn(x1_p, qc_p, mk_b, mv_b, w_co_b, tq, sp // tq, tail=x2_s)
        k_s.append(k.reshape(bs, ts, N_DIFF_HEADS, 2 * DIFF_HEAD_DIM))
        v_s.append(v.reshape(bs, ts, N_DIFF_HEADS, 2 * DIFF_HEAD_DIM))
        u_ext = jnp.concatenate([state_conv[l], u.reshape(bs, ts, cc)], axis=1)
        cs_s.append(u_ext[:, u_ext.shape[1] - halo:])

        bg = b_gu[l][:, 0::2].reshape(n_exp, 1, -1)
        bu = b_gu[l][:, 1::2].reshape(n_exp, 1, -1)
        yp, ysm = _moe(joint, n_p, row(norm_moe[l]), w_router[l].T, b_router[l].reshape(n_exp, 1), w_gu[l], bg, bu,
                       w_down[l], b_down[l].reshape(n_exp, 1, d), row(norm_final), l == depth - 1)
        xp, xs = yp, ysm

    st = lambda xs_: jnp.stack(xs_)
    return (xp.reshape(bp, sp, d), xs.reshape(bs, ts, d), st(k_p), st(v_p), st(cs_p), st(mk_p), st(mv_p), st(k_s),
            st(v_s), st(cs_s))
```

```python
import functools
import math

import jax
import jax.numpy as jnp
from jax import lax
from jax.experimental import pallas as pl
from jax.experimental.pallas import tpu as pltpu

F32 = jnp.float32
BF16 = jnp.bfloat16
I32 = jnp.int32

EPS = 1e-6
CHUNK = 64
N_DIFF_HEADS = 4
DIFF_HEAD_DIM = 64
DIFF_WIDTH = N_DIFF_HEADS * 2 * DIFF_HEAD_DIM
CONV_WIDTH = 31
N_CROSS_HEADS = 4
TOP_K = 4
SWIGLU_LIMIT = 7.0
SWIGLU_ALPHA = 1.702
NEG_INF = -1e30

V7X_LANES = 128
V7X_BF16_ROWS_PER_TILE = 16
V7X_VMEM_LIMIT_BYTES = 56 * 1024 * 1024

ROUTE_BLOCK = 256
PIECE_ROWS = 96
GROUP = V7X_BF16_ROWS_PER_TILE
FFN_TILE = 512
NO_SLOT = -(1 << 20)

_NT = (((1,), (1,)), ((), ()))
_TN = (((0,), (0,)), ((), ()))


def _cparams(*sem):
    return pltpu.CompilerParams(dimension_semantics=sem, vmem_limit_bytes=V7X_VMEM_LIMIT_BYTES)


def _sds(shape, dtype):
    return jax.ShapeDtypeStruct(shape, dtype)


def _rms(x, g):
    return x * lax.rsqrt(jnp.mean(x * x, axis=-1, keepdims=True) + EPS) * g


def _softmax_parts(pieces):
    m = functools.reduce(jnp.maximum, [jnp.max(s, axis=-1, keepdims=True) for s in pieces])
    es = [jnp.exp(s - m) for s in pieces]
    tot = functools.reduce(jnp.add, [jnp.sum(e, axis=-1, keepdims=True) for e in es])
    return es, 1.0 / tot


def _in_proj_body(x_ref, g_ref, w_ref, k_ref, v_ref, u_ref, qb_ref, kb_ref, vb_ref, *, cc):
    h = _rms(x_ref[...], g_ref[...]).astype(BF16)

    def proj(lo, n):
        return jnp.dot(h, w_ref[:, lo:lo + n], preferred_element_type=F32)

    dw = DIFF_WIDTH
    hw = 2 * DIFF_HEAD_DIM
    tm = x_ref.shape[0]

    def store_heads(ref, val):
        for hd in range(N_DIFF_HEADS):
            ref[pl.ds(hd, tm, stride=N_DIFF_HEADS), :] = val[:, hd * hw:(hd + 1) * hw]

    qb_ref[...] = (proj(0, dw) * (DIFF_HEAD_DIM ** -0.5)).astype(BF16)
    k = proj(dw, dw)
    store_heads(k_ref, k)
    kb_ref[...] = k.astype(BF16)
    v = proj(2 * dw, dw)
    store_heads(v_ref, v)
    vb_ref[...] = v.astype(BF16)
    u_ref[...] = proj(3 * dw, cc) * jax.nn.sigmoid(proj(3 * dw + cc, cc))


def _in_proj(x, g, w, tm):
    n, d = x.shape
    cols = w.shape[1]
    cc = (cols - 3 * DIFF_WIDTH) // 2
    hw = 2 * DIFF_HEAD_DIM
    row = lambda i: (i, 0)
    fix = lambda i: (0, 0)
    return pl.pallas_call(
        functools.partial(_in_proj_body, cc=cc),
        grid=(n // tm,),
        in_specs=[pl.BlockSpec((tm, d), row), pl.BlockSpec((1, d), fix), pl.BlockSpec((d, cols), fix)],
        out_specs=[pl.BlockSpec((tm * N_DIFF_HEADS, hw), row), pl.BlockSpec((tm * N_DIFF_HEADS, hw), row),
                   pl.BlockSpec((tm, cc), row),
                   pl.BlockSpec((tm, DIFF_WIDTH), row), pl.BlockSpec((tm, DIFF_WIDTH), row),
                   pl.BlockSpec((tm, DIFF_WIDTH), row)],
        out_shape=[_sds((n * N_DIFF_HEADS, hw), F32), _sds((n * N_DIFF_HEADS, hw), F32), _sds((n, cc), F32),
                   _sds((n, DIFF_WIDTH), BF16), _sds((n, DIFF_WIDTH), BF16), _sds((n, DIFF_WIDTH), BF16)],
        compiler_params=_cparams("arbitrary"),
        name="in_proj",
    )(x, g, w)


def _diff_lambda(lq1, lk1, lq2, lk2, lam_init):
    return (jnp.exp(jnp.sum(lq1[...] * lk1[...], axis=-1, keepdims=True))
            - jnp.exp(jnp.sum(lq2[...] * lk2[...], axis=-1, keepdims=True)) + lam_init)


def _split_components(q):
    lane = lax.broadcasted_iota(I32, q.shape, 1)
    zero = jnp.zeros_like(q)
    return jnp.where(lane < DIFF_HEAD_DIM, q, zero), jnp.where(lane >= DIFF_HEAD_DIM, q, zero)


def _sub_ln(o, sg, lam_init):
    o = o * lax.rsqrt(jnp.mean(o * o, axis=-1, keepdims=True) + EPS)
    return o * sg * (1.0 - lam_init)


def _diff_attn_prompt_body(lq1, lk1, lq2, lk2, sg_ref, q_ref, k_ref, v_ref, o_ref, *, tq, lam_init):
    lam = _diff_lambda(lq1, lk1, lq2, lk2, lam_init)
    seq = q_ref.shape[0]
    vis = (lax.broadcasted_iota(I32, (tq, tq), 1) // CHUNK) <= (lax.broadcasted_iota(I32, (tq, tq), 0) // CHUNK)
    for i in range(seq // tq):
        lo, ext = i * tq, (i + 1) * tq
        es, rs = [], []
        for qc in _split_components(q_ref[lo:ext, :]):
            s_diag = lax.dot_general(qc, k_ref[lo:ext, :], _NT, preferred_element_type=F32)
            pieces = [jnp.where(vis, s_diag, NEG_INF)]
            if i > 0:
                pieces.insert(0, lax.dot_general(qc, k_ref[0:lo, :], _NT, preferred_element_type=F32))
            e, r = _softmax_parts(pieces)
            es.append(e[0] if len(e) == 1 else jnp.concatenate(e, axis=1))
            rs.append(r)
        a = es[0] * rs[0] - es[1] * (lam * rs[1])
        o = jnp.dot(a.astype(BF16), v_ref[0:ext, :], preferred_element_type=F32)
        o_ref[lo:ext, :] = _sub_ln(o, sg_ref[...], lam_init).astype(o_ref.dtype)


def _diff_attn_prompt(lams, sg, qb, kb, vb, batch, seq, lam_init):
    hw = 2 * DIFF_HEAD_DIM
    tq = min(256, seq)
    blk = pl.BlockSpec((seq, hw), lambda b, h: (b, h))
    small = pl.BlockSpec((1, DIFF_HEAD_DIM), lambda b, h: (0, 0))
    return pl.pallas_call(
        functools.partial(_diff_attn_prompt_body, tq=tq, lam_init=lam_init),
        grid=(batch, N_DIFF_HEADS),
        in_specs=[small, small, small, small, pl.BlockSpec((1, hw), lambda b, h: (0, 0)), blk, blk, blk],
        out_specs=blk,
        out_shape=_sds((batch * seq, DIFF_WIDTH), BF16),
        compiler_params=_cparams("arbitrary", "arbitrary"),
        name="diff_attn_prompt",
    )(*lams, sg, qb, kb, vb)


def _diff_attn_sample_body(lq1, lk1, lq2, lk2, sg_ref, q_ref, k_ref, v_ref, ck_ref, cv_ref, o_ref, *, lam_init):
    lam = _diff_lambda(lq1, lk1, lq2, lk2, lam_init)
    t = q_ref.shape[0]
    past = ck_ref.shape[0] // N_DIFF_HEADS
    head_rows = pl.ds(pl.program_id(1), past, stride=N_DIFF_HEADS)
    ck = ck_ref[head_rows, :].astype(BF16)
    q_chunk = (past + lax.broadcasted_iota(I32, (t, 1), 0)) // CHUNK
    vis_past = (lax.broadcasted_iota(I32, (t, past), 1) // CHUNK) <= q_chunk
    vis_new = ((past + lax.broadcasted_iota(I32, (t, t), 1)) // CHUNK) <= q_chunk
    es, rs = [], []
    for qc in _split_components(q_ref[...]):
        s_past = lax.dot_general(qc, ck, _NT, preferred_element_type=F32)
        s_new = lax.dot_general(qc, k_ref[...], _NT, preferred_element_type=F32)
        e, r = _softmax_parts([jnp.where(vis_past, s_past, NEG_INF), jnp.where(vis_new, s_new, NEG_INF)])
        es.append(e)
        rs.append(r)
    a_past = es[0][0] * rs[0] - es[1][0] * (lam * rs[1])
    a_new = es[0][1] * rs[0] - es[1][1] * (lam * rs[1])
    o = (jnp.dot(a_past.astype(BF16), cv_ref[head_rows, :].astype(BF16), preferred_element_type=F32)
         + jnp.dot(a_new.astype(BF16), v_ref[...], preferred_element_type=F32))
    o_ref[...] = _sub_ln(o, sg_ref[...], lam_init).astype(o_ref.dtype)


def _diff_attn_sample(lams, sg, qb, kb, vb, cache_k, cache_v, batch, t, lam_init):
    hw = 2 * DIFF_HEAD_DIM
    rows = cache_k.shape[0] // batch
    blk = pl.BlockSpec((t, hw), lambda b, h: (b, h))
    cblk = pl.BlockSpec((rows, hw), lambda b, h: (b, 0))
    small = pl.BlockSpec((1, DIFF_HEAD_DIM), lambda b, h: (0, 0))
    return pl.pallas_call(
        functools.partial(_diff_attn_sample_body, lam_init=lam_init),
        grid=(batch, N_DIFF_HEADS),
        in_specs=[small, small, small, small, pl.BlockSpec((1, hw), lambda b, h: (0, 0)), blk, blk, blk, cblk, cblk],
        out_specs=blk,
        out_shape=_sds((batch * t, DIFF_WIDTH), BF16),
        compiler_params=_cparams("arbitrary", "arbitrary"),
        name="diff_attn_sample",
    )(*lams, sg, qb, kb, vb, cache_k, cache_v)


CONV_HALO = 32
CONV_ROWS = 32


def _conv_tail_body(u_ref, prev_ref, pre_ref, w_ref, cb_ref, lg_ref, lb_ref, o_ref, slab_ref, conv_ref):
    tc = u_ref.shape[0]
    first = pl.program_id(1) == 0
    off = CONV_HALO - (CONV_WIDTH - 1)
    n_slab = slab_ref.shape[0]
    for sl in range(n_slab):
        lanes = slice(sl * V7X_LANES, (sl + 1) * V7X_LANES)
        slab_ref[sl, 0:CONV_HALO, :] = jnp.where(first, pre_ref[:, lanes], prev_ref[:, lanes])
        slab_ref[sl, CONV_HALO:CONV_HALO + tc, :] = u_ref[:, lanes]

    def one_slab(sl, carry):
        acc = slab_ref[sl, off:off + tc, :] * w_ref[sl, 0:1, :]
        for j in range(1, CONV_WIDTH):
            acc = acc + slab_ref[sl, off + j:off + j + tc, :] * w_ref[sl, j:j + 1, :]
        conv_ref[sl] = acc
        return carry

    lax.fori_loop(0, n_slab, one_slab, 0)
    for r0 in range(0, tc, CONV_ROWS):
        c = jnp.concatenate([conv_ref[sl, r0:r0 + CONV_ROWS, :] for sl in range(n_slab)], axis=1) + cb_ref[...]
        mu = jnp.mean(c, axis=-1, keepdims=True)
        var = jnp.mean(jnp.square(c - mu), axis=-1, keepdims=True)
        cn = (c - mu) * lax.rsqrt(var + EPS) * lg_ref[...] + lb_ref[...]
        o_ref[r0:r0 + CONV_ROWS, :] = (cn * jax.nn.sigmoid(cn)).astype(o_ref.dtype)


def _conv_tail(u, prefix, w, cb, lg, lb, batch, t):
    c = u.shape[1]
    tc = min(256, t)
    nt = t // tc
    hb = tc // CONV_HALO
    n_slab = c // V7X_LANES
    fix = lambda b, i: (0, 0)
    return pl.pallas_call(
        _conv_tail_body,
        grid=(batch, nt),
        in_specs=[pl.BlockSpec((tc, c), lambda b, i: (b * nt + i, 0)),
                  pl.BlockSpec((CONV_HALO, c), lambda b, i: (jnp.maximum((b * nt + i) * hb - 1, 0), 0)),
                  pl.BlockSpec((None, CONV_HALO, c), lambda b, i: (b, 0, 0)),
                  pl.BlockSpec((n_slab, CONV_WIDTH, V7X_LANES), lambda b, i: (0, 0, 0)), pl.BlockSpec((1, c), fix),
                  pl.BlockSpec((1, c), fix), pl.BlockSpec((1, c), fix)],
        out_specs=pl.BlockSpec((tc, c), lambda b, i: (b * nt + i, 0)),
        out_shape=_sds((batch * t, c), BF16),
        scratch_shapes=[pltpu.VMEM((n_slab, CONV_HALO + tc, V7X_LANES), F32), pltpu.VMEM((n_slab, tc, V7X_LANES), F32)],
        compiler_params=_cparams("arbitrary", "arbitrary"),
        name="conv_tail",
    )(u, u, prefix, w.reshape(CONV_WIDTH, n_slab, V7X_LANES).transpose(1, 0, 2), cb, lg, lb)


def _mix_out_body(x_ref, ao_ref, co_ref, wo_ref, g_ref, wq_ref, x1_ref, qc_ref, *, scale):
    dw = ao_ref.shape[1]
    x1 = (x_ref[...] + jnp.dot(ao_ref[...], wo_ref[0:dw, :], preferred_element_type=F32)
          + jnp.dot(co_ref[...], wo_ref[dw:, :], preferred_element_type=F32))
    x1_ref[...] = x1
    h = _rms(x1, g_ref[...]).astype(BF16)
    qc_ref[...] = (jnp.dot(h, wq_ref[...], preferred_element_type=F32) * scale).astype(BF16)


def _mix_out(x, ao, co, wo, g, wq, tm):
    n, d = x.shape
    row = lambda i: (i, 0)
    fix = lambda i: (0, 0)
    scale = (d // N_CROSS_HEADS) ** -0.5
    return pl.pallas_call(
        functools.partial(_mix_out_body, scale=scale),
        grid=(n // tm,),
        in_specs=[pl.BlockSpec((tm, d), row), pl.BlockSpec((tm, ao.shape[1]), row), pl.BlockSpec((tm, co.shape[1]), row),
                  pl.BlockSpec(wo.shape, fix), pl.BlockSpec((1, d), fix), pl.BlockSpec(wq.shape, fix)],
        out_specs=[pl.BlockSpec((tm, d), row), pl.BlockSpec((tm, d), row)],
        out_shape=[_sds((n, d), F32), _sds((n, d), BF16)],
        compiler_params=_cparams("arbitrary"),
        name="mix_out",
    )(x, ao, co, wo, g, wq)


def _mem_kv_body(m_ref, g_ref, wk_ref, wv_ref, mk_ref, mv_ref, mkb_ref, mvb_ref):
    h = _rms(m_ref[...], g_ref[...]).astype(BF16)
    mk = jnp.dot(h, wk_ref[...], preferred_element_type=F32)
    mv = jnp.dot(h, wv_ref[...], preferred_element_type=F32)
    tm, d = m_ref.shape
    hd = d // N_CROSS_HEADS
    halves = hd // V7X_LANES

    def store_tiled(ref, val):
        for hh in range(N_CROSS_HEADS):
            for c in range(halves):
                col = hh * hd + c * V7X_LANES
                ref[pl.ds(c * N_CROSS_HEADS + hh, tm, stride=halves * N_CROSS_HEADS), :] = val[:, col:col + V7X_LANES]

    store_tiled(mk_ref, mk)
    store_tiled(mv_ref, mv)
    mkb_ref[...] = mk.astype(BF16)
    mvb_ref[...] = mv.astype(BF16)


def _mem_kv(mem, g, wk, wv, tm):
    n, d = mem.shape
    row = lambda i: (i, 0)
    fix = lambda i: (0, 0)
    return pl.pallas_call(
        _mem_kv_body,
        grid=(n // tm,),
        in_specs=[pl.BlockSpec((tm, d), row), pl.BlockSpec((1, d), fix), pl.BlockSpec(wk.shape, fix),
                  pl.BlockSpec(wv.shape, fix)],
        out_specs=[pl.BlockSpec((tm * d // V7X_LANES, V7X_LANES), row)] * 2 + [pl.BlockSpec((tm, d), row)] * 2,
        out_shape=[_sds((n * d // V7X_LANES, V7X_LANES), F32)] * 2 + [_sds((n, d), BF16)] * 2,
        compiler_params=_cparams("arbitrary"),
        name="mem_kv",
    )(mem, g, wk, wv)


def _cross_attn_body(x1_ref, qc_ref, mk_ref, mv_ref, wo_ref, *rest, n_own, tiled):
    x2_ref = rest[-1]

    @pl.when(pl.program_id(0) < n_own)
    def _():
        hd = qc_ref.shape[1] // N_CROSS_HEADS
        halves = hd // V7X_LANES

        def head(ref, h):
            if not tiled:
                return ref[:, h * hd:(h + 1) * hd].astype(BF16)
            m = ref.shape[0] // (halves * N_CROSS_HEADS)
            parts = [ref[pl.ds(c * N_CROSS_HEADS + h, m, stride=halves * N_CROSS_HEADS), :] for c in range(halves)]
            return jnp.concatenate(parts, axis=1).astype(BF16)

        outs = []
        for h in range(N_CROSS_HEADS):
            sl = slice(h * hd, (h + 1) * hd)
            s = lax.dot_general(qc_ref[:, sl], head(mk_ref, h), _NT, preferred_element_type=F32)
            (e,), r = _softmax_parts([s])
            outs.append(jnp.dot((e * r).astype(BF16), head(mv_ref, h), preferred_element_type=F32))
        o = jnp.concatenate(outs, axis=1).astype(BF16)
        x2_ref[...] = x1_ref[...] + jnp.dot(o, wo_ref[...], preferred_element_type=F32)

    if len(rest) == 2:
        @pl.when(pl.program_id(0) >= n_own)
        def _():
            x2_ref[...] = rest[0][...]


def _cross_attn(x1, qc, mk, mv, wo, tm, tiles_per_batch, tail=None):
    n, d = x1.shape
    tiled = mk.shape[2] != d
    n_own = n // tm
    n_tail = 0 if tail is None else tail.shape[0] // tm
    own = lambda i: (jnp.minimum(i, n_own - 1), 0)
    fix = lambda i: (0, 0)
    mem = pl.BlockSpec((None,) + mk.shape[1:], lambda i: (jnp.minimum(i, n_own - 1) // tiles_per_batch, 0, 0))
    in_specs = [pl.BlockSpec((tm, d), own), pl.BlockSpec((tm, d), own), mem, mem, pl.BlockSpec(wo.shape, fix)]
    args = [x1, qc, mk, mv, wo]
    if tail is not None:
        in_specs.append(pl.BlockSpec((tm, d), lambda i: (jnp.maximum(i - n_own, 0), 0)))
        args.append(tail)
    return pl.pallas_call(
        functools.partial(_cross_attn_body, n_own=n_own, tiled=tiled),
        grid=(n_own + n_tail,),
        in_specs=in_specs,
        out_specs=pl.BlockSpec((tm, d), lambda i: (i, 0)),
        out_shape=_sds((n + n_tail * tm, d), F32),
        compiler_params=_cparams("arbitrary"),
        name="cross_attn",
    )(*args)


def _router_body(x_ref, g_ref, wr_ref, br_ref, hm_ref, d_ref, gt_ref, cnt_ref, brel_ref, tot_ref, cur_ref):
    @pl.when(pl.program_id(0) == 0)
    def _():
        cur_ref[...] = jnp.zeros_like(cur_ref)

    n_exp = wr_ref.shape[0]
    tb = x_ref.shape[0]
    hm = _rms(x_ref[...], g_ref[...])
    hm_hi = hm.astype(BF16)
    hm_ref[...] = hm_hi
    hm_lo = (hm - hm_hi.astype(F32)).astype(BF16)
    wr = wr_ref[...]
    wr_hi = wr.astype(BF16)
    wr_lo = (wr - wr_hi.astype(F32)).astype(BF16)
    nt_dot = lambda a, b: lax.dot_general(a, b, _NT, preferred_element_type=F32)
    both = nt_dot(jnp.concatenate([wr_hi, wr_lo], axis=0), hm_hi)
    logits = both[:n_exp] + both[n_exp:] + nt_dot(wr_hi, hm_lo) + br_ref[...]
    eidx = lax.broadcasted_iota(I32, (n_exp, tb), 0)
    work = logits
    sel = None
    top = None
    for k in range(TOP_K):
        m = jnp.max(work, axis=0, keepdims=True)
        top = m if k == 0 else top
        first = jnp.min(jnp.where(work == m, eidx, n_exp), axis=0, keepdims=True)
        pick = eidx == first
        sel = pick if k == 0 else jnp.logical_or(sel, pick)
        work = jnp.where(pick, -jnp.inf, work)
    e = jnp.where(sel, jnp.exp(logits - top), 0.0)
    gt_ref[...] = e * (1.0 / jnp.sum(e, axis=0, keepdims=True))
    self32 = jnp.where(sel, 1.0, 0.0)
    before = jnp.where(lax.broadcasted_iota(I32, (tb, tb), 0) < lax.broadcasted_iota(I32, (tb, tb), 1), 1.0, 0.0)
    rank = jnp.dot(self32.astype(BF16), before.astype(BF16), preferred_element_type=F32).astype(I32)
    d_ref[...] = jnp.where(sel, rank, NO_SLOT)
    cnt = jnp.broadcast_to(jnp.sum(self32, axis=1, keepdims=True).astype(I32), cnt_ref.shape)
    cnt_ref[...] = cnt
    brel_ref[...] = cur_ref[...]
    cur_ref[...] = cur_ref[...] + cnt
    tot_ref[...] = cur_ref[...]


def _router(x2, g, wr_t, br):
    n, d = x2.shape
    n_exp = wr_t.shape[0]
    tb = ROUTE_BLOCK
    nb = n // tb
    fix = lambda i: (0, 0)
    blk3 = lambda w: pl.BlockSpec((None, n_exp, w), lambda i: (i, 0, 0))
    return pl.pallas_call(
        _router_body,
        grid=(nb,),
        in_specs=[pl.BlockSpec((tb, d), lambda i: (i, 0)), pl.BlockSpec((1, d), fix), pl.BlockSpec((n_exp, d), fix),
                  pl.BlockSpec((n_exp, 1), fix)],
        out_specs=[pl.BlockSpec((tb, d), lambda i: (i, 0)), blk3(tb), blk3(tb), blk3(V7X_LANES), blk3(V7X_LANES),
                   pl.BlockSpec((n_exp, V7X_LANES), fix)],
        out_shape=[_sds((n, d), BF16), _sds((nb, n_exp, tb), I32), _sds((nb, n_exp, tb), F32),
                   _sds((nb, n_exp, V7X_LANES), I32), _sds((nb, n_exp, V7X_LANES), I32), _sds((n_exp, V7X_LANES), I32)],
        scratch_shapes=[pltpu.VMEM((n_exp, V7X_LANES), I32)],
        compiler_params=_cparams("arbitrary"),
        name="moe_router",
    )(x2, g, wr_t, br)


def _region_tiles(total):
    return (total + PIECE_ROWS + FFN_TILE - 1) // FFN_TILE


def _plan_body(tot_ref, start_ref, tile_e_ref, nt_ref, *, n_exp, max_tiles):
    shift = int(math.log2(FFN_TILE))

    def per_expert(e, pos):
        start_ref[e] = pos * FFN_TILE
        nt = lax.shift_right_logical(tot_ref[e] + (PIECE_ROWS + FFN_TILE - 1), shift)

        def mark(j, c):
            tile_e_ref[pos + j] = e
            return c

        lax.fori_loop(0, nt, mark, 0)
        return pos + nt

    n = lax.fori_loop(0, n_exp, per_expert, jnp.int32(0))
    nt_ref[0] = n

    def fill(j, c):
        tile_e_ref[j] = n_exp - 1
        return c

    lax.fori_loop(n, max_tiles, fill, 0)


def _plan(tot, max_tiles):
    n_exp = tot.shape[0]
    smem = pl.BlockSpec(memory_space=pltpu.SMEM)
    return pl.pallas_call(
        functools.partial(_plan_body, n_exp=n_exp, max_tiles=max_tiles),
        in_specs=[smem],
        out_specs=[smem, smem, smem],
        out_shape=[_sds((n_exp,), I32), _sds((max_tiles,), I32), _sds((1,), I32)],
        name="moe_plan",
    )(tot)


def _window(start_ref, meta_ref, e, n_exp):
    base = start_ref[e] + meta_ref[0, e]
    return base, jnp.bitwise_and(base, GROUP - 1), meta_ref[0, n_exp + e]


def _num_rounds(start_ref, meta_ref, n_exp):
    ends = [_window(start_ref, meta_ref, e, n_exp) for e in range(n_exp)]
    longest = functools.reduce(jnp.maximum, [s + n for _, s, n in ends])
    return lax.div(longest + (PIECE_ROWS - 1), jnp.int32(PIECE_ROWS))


def _perm_rows(d_ref, start_ref, meta_ref, r, e, n_exp):
    tb = d_ref.shape[1]
    _, s, _ = _window(start_ref, meta_ref, e, n_exp)
    tgt = d_ref[e:e + 1, :] + (s - PIECE_ROWS * r)
    return lax.broadcasted_iota(I32, (PIECE_ROWS, tb), 0) == tgt


EXPERTS_PER_DOT = 8
COMBINE_HEAD_ROWS = 64


def _dispatch_body(start_ref, tot_ref, hm_ref, d_ref, meta_ref, xs_ref, pieces_ref, perm_ref, carry_ref, zero_ref,
                   pend_ref, sem, *, n_exp):
    b = pl.program_id(0)
    p_rows, grp = PIECE_ROWS, GROUP
    d = hm_ref.shape[1]
    win = functools.partial(_window, start_ref, meta_ref, n_exp=n_exp)

    @pl.when(b == 0)
    def _():
        carry_ref[...] = jnp.zeros_like(carry_ref)
        zero_ref[...] = jnp.zeros_like(zero_ref)
        pieces_ref[...] = jnp.zeros_like(pieces_ref)
        perm_ref[...] = jnp.zeros_like(perm_ref)

        def n_fill(e):
            region_end = start_ref[e] + _region_tiles(tot_ref[e]) * FFN_TILE
            z0 = start_ref[e] + jnp.bitwise_and(tot_ref[e] + (grp - 1), -grp)
            return z0, lax.div(region_end - z0 + (p_rows - 1), jnp.int32(p_rows))

        def fill_copy(e, q):
            z0, _ = n_fill(e)
            return pltpu.make_async_copy(zero_ref.at[pl.ds(0, p_rows)],
                                         xs_ref.at[pl.ds(pl.multiple_of(z0 + q * p_rows, grp), p_rows)], sem.at[0])

        def start_e(e, c):
            lax.fori_loop(0, n_fill(e)[1], lambda q, c2: (fill_copy(e, q).start(), c2)[1], 0)
            return c

        def wait_e(e, c):
            lax.fori_loop(0, n_fill(e)[1], lambda q, c2: (fill_copy(e, q).wait(), c2)[1], 0)
            return c

        lax.fori_loop(0, n_exp, start_e, 0)
        lax.fori_loop(0, n_exp, wait_e, 0)

        used = lax.shift_right_logical(start_ref[n_exp - 1], int(math.log2(FFN_TILE))) + _region_tiles(tot_ref[n_exp - 1])
        total_tiles = (xs_ref.shape[0] - p_rows) // FFN_TILE

        def tail_copy(t):
            return pltpu.make_async_copy(zero_ref, xs_ref.at[pl.ds(pl.multiple_of(t * FFN_TILE, FFN_TILE), FFN_TILE)],
                                         sem.at[0])

        slack_copy = pltpu.make_async_copy(zero_ref.at[pl.ds(0, p_rows)],
                                           xs_ref.at[pl.ds(total_tiles * FFN_TILE, p_rows)], sem.at[0])
        lax.fori_loop(used, total_tiles, lambda t, c: (tail_copy(t).start(), c)[1], 0)
        slack_copy.start()
        lax.fori_loop(used, total_tiles, lambda t, c: (tail_copy(t).wait(), c)[1], 0)
        slack_copy.wait()

        pend_ref[0] = 0

    slot = lax.rem(b, 2)
    buf = pieces_ref.at[slot]
    log_g = int(math.log2(grp))

    def piece_copy(sl, e, r, sem_ref):
        base, s, _ = win(e)
        dst = pl.multiple_of(base - s + r * p_rows, grp)
        return pltpu.make_async_copy(pieces_ref.at[sl, pl.ds(pl.multiple_of(e * p_rows, p_rows), p_rows)],
                                     xs_ref.at[pl.ds(dst, p_rows)], sem_ref)

    def fill_pieces(r):
        for g0 in range(0, n_exp, EXPERTS_PER_DOT):
            pm = jnp.concatenate(
                [jnp.where(_perm_rows(d_ref, start_ref, meta_ref, r, e, n_exp), 1.0, 0.0).astype(BF16)
                 for e in range(g0, g0 + EXPERTS_PER_DOT)], axis=0)
            buf[g0 * p_rows:(g0 + EXPERTS_PER_DOT) * p_rows, :] = jnp.dot(
                pm, hm_ref[...], preferred_element_type=F32).astype(BF16)

    def merge_carry(e, first_row):
        _, s, n = win(e)
        rows = pl.ds(first_row, grp)
        crow = slice(e * grp, (e + 1) * grp)
        keep = lax.broadcasted_iota(I32, (grp, d), 0) < s
        buf[rows, :] = jnp.where(keep, carry_ref[crow, :], buf[rows, :])
        return crow, s + n

    half = p_rows // 2

    def half_copy(sl, src_row, dst_row):
        aligned = lambda r: r if isinstance(r, int) else pl.multiple_of(r, grp)
        return pltpu.make_async_copy(pieces_ref.at[sl, pl.ds(aligned(src_row), half)],
                                     xs_ref.at[pl.ds(aligned(dst_row), half)], sem.at[1])

    def window_copies(e, src_row, always_both):
        base, s, n = win(e)
        dst = base - s
        half_copy(slot, src_row, dst).start()
        if always_both:
            half_copy(slot, src_row + half, dst + half).start()
            return 2
        both = s + n > half

        @pl.when(both)
        def _():
            half_copy(slot, src_row + half, dst + half).start()

        return 1 + both.astype(I32)

    def wait_previous():
        lax.fori_loop(0, pend_ref[0], lambda i, c: (half_copy(0, 0, 0).wait(), c)[1], 0)

    n_rounds = _num_rounds(start_ref, meta_ref, n_exp)

    @pl.when(n_rounds == 1)
    def _():
        firsts = []
        nxt = jnp.int32(0)
        for e in range(n_exp):
            _, s, n = win(e)
            firsts.append(pl.multiple_of(nxt, grp))
            nxt = nxt + jnp.bitwise_and(s + n + (grp - 1), -grp)
        for e in range(n_exp):
            perm_ref[pl.ds(firsts[e], p_rows), :] = jnp.where(
                _perm_rows(d_ref, start_ref, meta_ref, 0, e, n_exp), 1.0, 0.0).astype(BF16)
        step = EXPERTS_PER_DOT * PIECE_ROWS
        packed_rows = perm_ref.shape[0] - p_rows
        for r0 in range(0, packed_rows, step):
            r1 = min(r0 + step, packed_rows)
            buf[r0:r1, :] = jnp.dot(perm_ref[r0:r1, :], hm_ref[...], preferred_element_type=F32).astype(BF16)
        for e in range(n_exp):
            crow, end = merge_carry(e, firsts[e])
            last = buf[pl.ds(pl.multiple_of(firsts[e] + jnp.bitwise_and(end, -grp), grp), grp), :]
            carry_ref[crow, :] = jnp.where(jnp.bitwise_and(end, grp - 1) != 0, last, jnp.zeros_like(last))
        wait_previous()
        pend_ref[0] = functools.reduce(jnp.add, [window_copies(e, firsts[e], False) for e in range(n_exp)])

    def extra_round(r, c):
        fill_pieces(r)

        def start_e(e, c2):
            _, s, n = win(e)

            @pl.when(lax.div(s + n, jnp.int32(p_rows)) == r)
            def _():
                g = lax.shift_right_logical(lax.rem(s + n, jnp.int32(p_rows)), log_g)
                carry_ref[pl.ds(pl.multiple_of(e * grp, grp), grp), :] = buf[
                    pl.ds(pl.multiple_of(e * p_rows + g * grp, grp), grp), :]

            @pl.when(s + n > r * p_rows)
            def _():
                piece_copy(slot, e, r, sem.at[0]).start()

            return c2

        def wait_e(e, c2):
            _, s, n = win(e)

            @pl.when(s + n > r * p_rows)
            def _():
                piece_copy(slot, e, r, sem.at[0]).wait()

            return c2

        lax.fori_loop(0, n_exp, start_e, 0)
        lax.fori_loop(0, n_exp, wait_e, 0)
        return c

    def drain():
        wait_previous()
        pend_ref[0] = 0

    @pl.when(n_rounds > 1)
    def _():
        fill_pieces(0)
        for e in range(n_exp):
            crow, end = merge_carry(e, e * p_rows)
            g = jnp.minimum(lax.shift_right_logical(end, log_g), p_rows // grp - 1)
            last = buf[pl.ds(pl.multiple_of(e * p_rows + g * grp, grp), grp), :]
            carry_ref[crow, :] = jnp.where(end < p_rows, last, jnp.zeros_like(last))
        wait_previous()
        pend_ref[0] = sum(window_copies(e, e * p_rows, True) for e in range(n_exp))
        drain()
        lax.fori_loop(1, n_rounds, extra_round, 0)

    @pl.when(b == pl.num_programs(0) - 1)
    def _():
        drain()


def _dispatch(start, tot, hm, dmat, meta, rows_max):
    n, d = hm.shape
    nb, n_exp, tb = dmat.shape
    packed_rows = -(-(TOP_K * tb + 2 * (GROUP - 1) * n_exp) // 256) * 256
    assert packed_rows + PIECE_ROWS <= n_exp * PIECE_ROWS
    grid_spec = pltpu.PrefetchScalarGridSpec(
        num_scalar_prefetch=2,
        grid=(nb,),
        in_specs=[pl.BlockSpec((tb, d), lambda i, *_: (i, 0)), pl.BlockSpec((None, n_exp, tb), lambda i, *_: (i, 0, 0)),
                  pl.BlockSpec((None, 1, 2 * n_exp), lambda i, *_: (i, 0, 0), memory_space=pltpu.SMEM)],
        out_specs=pl.BlockSpec(memory_space=pl.ANY),
        scratch_shapes=[pltpu.VMEM((2, n_exp * PIECE_ROWS, d), BF16), pltpu.VMEM((packed_rows + PIECE_ROWS, tb), BF16),
                        pltpu.VMEM((n_exp * GROUP, d), BF16), pltpu.VMEM((FFN_TILE, d), BF16), pltpu.SMEM((1,), I32),
                        pltpu.SemaphoreType.DMA((2,))],
    )
    return pl.pallas_call(
        functools.partial(_dispatch_body, n_exp=n_exp),
        grid_spec=grid_spec,
        out_shape=_sds((rows_max, d), BF16),
        compiler_params=_cparams("arbitrary"),
        name="moe_dispatch",
    )(start, tot, hm, dmat, meta)


def _combine_body(start_ref, x2_ref, d_ref, gt_ref, gf_ref, meta_ref, next_ref, ys_ref, yp_ref, ysm_ref, pieces_ref,
                  acc_ref, fetched_ref, sem, *, n_exp, nb_prompt, final):
    b = pl.program_id(0)
    nb = pl.num_programs(0)
    p_rows, grp = PIECE_ROWS, GROUP
    win = functools.partial(_window, start_ref, meta_ref, n_exp=n_exp)
    slot = lax.rem(b, 2)
    buf = pieces_ref.at[slot]

    def piece_copy(m_ref, sl, e, r, sem_ref):
        base, s, _ = _window(start_ref, m_ref, e, n_exp)
        src = pl.multiple_of(base - s + r * p_rows, grp)
        return pltpu.make_async_copy(ys_ref.at[pl.ds(src, p_rows)],
                                     pieces_ref.at[sl, pl.ds(pl.multiple_of(e * p_rows, p_rows), p_rows)], sem_ref)

    head = COMBINE_HEAD_ROWS
    rest = p_rows - head

    def part_copy(m_ref, sl, e, second, sem_ref):
        base, s, _ = _window(start_ref, m_ref, e, n_exp)
        off, rows = (head, rest) if second else (0, head)
        return pltpu.make_async_copy(ys_ref.at[pl.ds(pl.multiple_of(base - s + off, grp), rows)],
                                     pieces_ref.at[sl, pl.ds(e * p_rows + off, rows)], sem_ref)

    def fetch_windows(m_ref, sl, sem_ref):
        tails = jnp.int32(0)
        for e in range(n_exp):
            _, s, n = _window(start_ref, m_ref, e, n_exp)
            part_copy(m_ref, sl, e, False, sem_ref).start()
            long = s + n > head

            @pl.when(long)
            def _():
                part_copy(m_ref, sl, e, True, sem_ref).start()

            tails = tails + long.astype(I32)
        fetched_ref[sl] = tails

    @pl.when(b == 0)
    def _():
        pieces_ref[...] = jnp.zeros_like(pieces_ref)
        fetch_windows(meta_ref, slot, sem.at[1 + slot])

    @pl.when(b + 1 < nb)
    def _():
        fetch_windows(next_ref, 1 - slot, sem.at[2 - slot])

    for e in range(n_exp):
        part_copy(meta_ref, slot, e, False, sem.at[1 + slot]).wait()
    lax.fori_loop(0, fetched_ref[slot], lambda i, c: (part_copy(meta_ref, slot, 0, True, sem.at[1 + slot]).wait(), c)[1],
                  0)

    def scatter(r):
        parts = []
        for g0 in range(0, n_exp, EXPERTS_PER_DOT):
            pms, ys = [], []
            for e in range(g0, g0 + EXPERTS_PER_DOT):
                hit = _perm_rows(d_ref, start_ref, meta_ref, r, e, n_exp)
                gate = jnp.sum(jnp.where(hit, gt_ref[e:e + 1, :], 0.0), axis=1, keepdims=True)
                pms.append(jnp.where(hit, 1.0, 0.0).astype(BF16))
                ys.append((buf[e * p_rows:(e + 1) * p_rows, :].astype(F32) * gate).astype(BF16))
            parts.append(lax.dot_general(jnp.concatenate(pms, axis=0), jnp.concatenate(ys, axis=0), _TN,
                                         preferred_element_type=F32))
        return functools.reduce(jnp.add, parts)

    acc_ref[...] = x2_ref[...] + scatter(0)

    def extra_round(r, c):
        def start_e(e, c2):
            _, s, n = win(e)

            @pl.when(s + n > r * p_rows)
            def _():
                piece_copy(meta_ref, slot, e, r, sem.at[0]).start()

            return c2

        def wait_e(e, c2):
            _, s, n = win(e)

            @pl.when(s + n > r * p_rows)
            def _():
                piece_copy(meta_ref, slot, e, r, sem.at[0]).wait()

            return c2

        lax.fori_loop(0, n_exp, start_e, 0)
        lax.fori_loop(0, n_exp, wait_e, 0)
        acc_ref[...] += scatter(r)
        return c

    lax.fori_loop(1, _num_rounds(start_ref, meta_ref, n_exp), extra_round, 0)
    y = _rms(acc_ref[...], gf_ref[...]) if final else acc_ref[...]

    @pl.when(b < nb_prompt)
    def _():
        yp_ref[...] = y

    @pl.when(b >= nb_prompt)
    def _():
        ysm_ref[...] = y


def _combine(start, x2, dmat, gates, gf, meta, ys, n_prompt, final):
    n, d = x2.shape
    nb, n_exp, tb = dmat.shape
    nbp = n_prompt // tb
    blk3 = pl.BlockSpec((None, n_exp, tb), lambda i, *_: (i, 0, 0))
    grid_spec = pltpu.PrefetchScalarGridSpec(
        num_scalar_prefetch=1,
        grid=(nb,),
        in_specs=[pl.BlockSpec((tb, d), lambda i, *_: (i, 0)), blk3, blk3, pl.BlockSpec((1, d), lambda i, *_: (0, 0)),
                  pl.BlockSpec((None, 1, 2 * n_exp), lambda i, *_: (i, 0, 0), memory_space=pltpu.SMEM),
                  pl.BlockSpec((None, 1, 2 * n_exp), lambda i, *_: (jnp.minimum(i + 1, nb - 1), 0, 0),
                               memory_space=pltpu.SMEM),
                  pl.BlockSpec(memory_space=pl.ANY)],
        out_specs=[pl.BlockSpec((tb, d), lambda i, *_: (jnp.minimum(i, nbp - 1), 0)),
                   pl.BlockSpec((tb, d), lambda i, *_: (jnp.maximum(i - nbp, 0), 0))],
        scratch_shapes=[pltpu.VMEM((2, n_exp * PIECE_ROWS, d), BF16), pltpu.VMEM((tb, d), F32), pltpu.SMEM((2,), I32),
                        pltpu.SemaphoreType.DMA((3,))],
    )
    return pl.pallas_call(
        functools.partial(_combine_body, n_exp=n_exp, nb_prompt=nbp, final=final),
        grid_spec=grid_spec,
        out_shape=[_sds((n_prompt, d), F32), _sds((n - n_prompt, d), F32)],
        compiler_params=_cparams("arbitrary"),
        name="moe_combine",
    )(start, x2, dmat, gates, gf, meta, meta, ys)


SPLIT = 2 * V7X_LANES


def _ffn_body(tile_e_ref, nt_ref, x_ref, wgu_ref, bg_ref, bu_ref, wd_ref, bd_ref, y_ref, wgu_s, wd_s):
    i = pl.program_id(0)
    live = i < nt_ref[0]
    half = SPLIT // 2

    @pl.when(jnp.logical_not(live))
    def _():
        y_ref[...] = jnp.zeros_like(y_ref)

    @pl.when(jnp.logical_and(live, jnp.logical_or(i == 0, tile_e_ref[i] != tile_e_ref[jnp.maximum(i - 1, 0)])))
    def _():
        r = lax.broadcasted_iota(I32, (SPLIT, SPLIT), 0)
        c = lax.broadcasted_iota(I32, (SPLIT, SPLIT), 1)
        pick = jnp.where(r == jnp.where(c < half, 2 * c, 2 * (c - half) + 1), 1.0, 0.0).astype(BF16)
        for j in range(wgu_ref.shape[1] // SPLIT):
            blk = wgu_ref[:, j * SPLIT:(j + 1) * SPLIT].astype(BF16)
            wgu_s[:, j * SPLIT:(j + 1) * SPLIT] = jnp.dot(blk, pick, preferred_element_type=F32).astype(BF16)
        wd_s[...] = wd_ref[...].astype(BF16)

    @pl.when(live)
    def _():
        x = x_ref[...]
        acts = []
        for j in range(wgu_s.shape[1] // SPLIT):
            gu = jnp.dot(x, wgu_s[:, j * SPLIT:(j + 1) * SPLIT], preferred_element_type=F32)
            gate = jnp.minimum(gu[:, :half] + bg_ref[:, j * half:(j + 1) * half], SWIGLU_LIMIT)
            up = jnp.clip(gu[:, half:] + bu_ref[:, j * half:(j + 1) * half], -SWIGLU_LIMIT, SWIGLU_LIMIT)
            acts.append(((up + 1.0) * gate * jax.nn.sigmoid(SWIGLU_ALPHA * gate)).astype(BF16))
        act = jnp.concatenate(acts, axis=1)
        y_ref[...] = (jnp.dot(act, wd_s[...], preferred_element_type=F32) + bd_ref[...]).astype(y_ref.dtype)


def _ffn(tile_e, nt, xs, wgu, bg, bu, wd, bd):
    d = xs.shape[1]
    f = wd.shape[1]
    max_tiles = tile_e.shape[0]
    tile = lambda i, te, nt: (jnp.minimum(i, nt[0] - 1), 0)
    wsel = lambda i, te, nt: (te[i], 0, 0)
    grid_spec = pltpu.PrefetchScalarGridSpec(
        num_scalar_prefetch=2,
        grid=(max_tiles,),
        in_specs=[pl.BlockSpec((FFN_TILE, d), tile), pl.BlockSpec((None, d, 2 * f), wsel),
                  pl.BlockSpec((None, 1, f), wsel), pl.BlockSpec((None, 1, f), wsel), pl.BlockSpec((None, f, d), wsel),
                  pl.BlockSpec((None, 1, d), wsel)],
        out_specs=pl.BlockSpec((FFN_TILE, d), lambda i, te, nt: (i, 0)),
        scratch_shapes=[pltpu.VMEM((d, 2 * f), BF16), pltpu.VMEM((f, d), BF16)],
    )
    return pl.pallas_call(
        _ffn_body,
        grid_spec=grid_spec,
        out_shape=_sds((max_tiles * FFN_TILE, d), BF16),
        compiler_params=_cparams("arbitrary"),
        name="moe_ffn",
    )(tile_e, nt, xs, wgu, bg, bu, wd, bd)


def _moe(x2, n_prompt, g_moe, wr_t, br, wgu, bg, bu, wd, bd, g_final, final):
    n, d = x2.shape
    n_exp = wr_t.shape[0]
    hm, dmat, gates, cnt, brel, tot = _router(x2, g_moe, wr_t, br)
    meta = jnp.concatenate([brel[:, :, 0], cnt[:, :, 0]], axis=1).reshape(-1, 1, 2 * n_exp)
    tot = tot[:, 0]
    max_tiles = (TOP_K * n + n_exp * (PIECE_ROWS + FFN_TILE - 1)) // FFN_TILE + 1
    start, tile_e, nt = _plan(tot, max_tiles)
    rows_max = max_tiles * FFN_TILE + PIECE_ROWS
    xs = _dispatch(start, tot, hm, dmat, meta, rows_max)
    ys = _ffn(tile_e, nt, xs, wgu, bg, bu, wd, bd)
    return _combine(start, x2, dmat, gates, g_final, meta, ys, n_prompt, final)


def _lambda_init(layer):
    return 0.8 - 0.6 * math.exp(-0.3 * layer)


def kernel(x_prompt, x_sample, cache_k, cache_v, state_conv, cache_mem_k, cache_mem_v, mem_prompt, norm_mix, w_in, lambda_q1, lambda_k1, lambda_q2, lambda_k2, subln_g, conv_w, conv_b, conv_ln_g, conv_ln_b, w_out, norm_cross, norm_mem, w_cq, w_ck, w_cv, w_co, norm_moe, w_router, b_router, w_gu, b_gu, w_down, b_down, norm_final):
    depth = w_in.shape[0]
    bp, sp, d = x_prompt.shape
    bs, ts, _ = x_sample.shape
    n_p, n_s = bp * sp, bs * ts
    n_mem = mem_prompt.shape[1]
    n_exp = w_router.shape[2]
    cc = conv_w.shape[2]
    halo = CONV_WIDTH - 1
    row = lambda a: a.reshape(1, -1)

    assert n_p % ROUTE_BLOCK == 0 and n_s % ROUTE_BLOCK == 0, "token counts must fill whole routing blocks"
    assert sp % 512 == 0 and ts >= halo and ts % 8 == 0, "unsupported sequence lengths"
    xp = x_prompt.reshape(n_p, d)
    xs = x_sample.reshape(n_s, d)
    k_p, v_p, cs_p, mk_p, mv_p, k_s, v_s, cs_s = [], [], [], [], [], [], [], []
    for l in range(depth):
        lam_init = _lambda_init(l)
        lams = [row(lambda_q1[l]), row(lambda_k1[l]), row(lambda_q2[l]), row(lambda_k2[l])]
        sg = row(subln_g[l])
        w_in_b = w_in[l].astype(BF16)
        w_out_b = w_out[l].astype(BF16)
        w_cq_b, w_ck_b, w_cv_b, w_co_b = (w[l].astype(BF16) for w in (w_cq, w_ck, w_cv, w_co))
        conv_args = (conv_w[l], row(conv_b[l]), row(conv_ln_g[l]), row(conv_ln_b[l]))

        k, v, u, qb, kb, vb = _in_proj(xp, row(norm_mix[l]), w_in_b, 512)
        ao = _diff_attn_prompt(lams, sg, qb, kb, vb, bp, sp, lam_init)
        co = _conv_tail(u, jnp.zeros((bp, CONV_HALO, cc), F32), *conv_args, bp, sp)
        x1, qc = _mix_out(xp, ao, co, w_out_b, row(norm_cross[l]), w_cq_b, 512)
        mk, mv, mkb, mvb = _mem_kv(mem_prompt.reshape(bp * n_mem, d), row(norm_mem[l]), w_ck_b, w_cv_b, n_mem)
        tq = math.gcd(512, sp, n_s)
        x1_p, qc_p, mk_b, mv_b = x1, qc, mkb.reshape(bp, n_mem, d), mvb.reshape(bp, n_mem, d)
        k_p.append(k.reshape(bp, sp, N_DIFF_HEADS, 2 * DIFF_HEAD_DIM))
        v_p.append(v.reshape(bp, sp, N_DIFF_HEADS, 2 * DIFF_HEAD_DIM))
        cs_p.append(u.reshape(bp, sp, cc)[:, sp - halo:])
        hd_c = d // N_CROSS_HEADS
        untile = lambda a: (a.reshape(bp, n_mem, hd_c // V7X_LANES, N_CROSS_HEADS, V7X_LANES)
                            .transpose(0, 1, 3, 2, 4).reshape(bp, n_mem, N_CROSS_HEADS, hd_c))
        mk_p.append(untile(mk))
        mv_p.append(untile(mv))

        k, v, u, qb, kb, vb = _in_proj(xs, row(norm_mix[l]), w_in_b, min(512, n_s))
        ao = _diff_attn_sample(lams, sg, qb, kb, vb, cache_k[l].reshape(-1, 2 * DIFF_HEAD_DIM),
                               cache_v[l].reshape(-1, 2 * DIFF_HEAD_DIM), bs, ts, lam_init)
        prefix = jnp.pad(state_conv[l], ((0, 0), (CONV_HALO - halo, 0), (0, 0)))
        co = _conv_tail(u, prefix, *conv_args, bs, ts)
        x1, qc = _mix_out(xs, ao, co, w_out_b, row(norm_cross[l]), w_cq_b, min(512, n_s))
        hd_c = d // N_CROSS_HEADS
        tile = lambda a: (a.reshape(bs, n_mem, N_CROSS_HEADS, hd_c // V7X_LANES, V7X_LANES)
                          .transpose(0, 1, 3, 2, 4).reshape(bs, n_mem * d // V7X_LANES, V7X_LANES))
        x2_s = _cross_attn(x1, qc, tile(cache_mem_k[l]), tile(cache_mem_v[l]), w_co_b, ts, 1)
        joint = _cross_attn(x1_p, qc_p, mk_b, mv_b, w_co_b, tq, sp // tq, tail=x2_s)
        k_s.append(k.reshape(bs, ts, N_DIFF_HEADS, 2 * DIFF_HEAD_DIM))
        v_s.append(v.reshape(bs, ts, N_DIFF_HEADS, 2 * DIFF_HEAD_DIM))
        u_ext = jnp.concatenate([state_conv[l], u.reshape(bs, ts, cc)], axis=1)
        cs_s.append(u_ext[:, u_ext.shape[1] - halo:])

        bg = b_gu[l][:, 0::2].reshape(n_exp, 1, -1)
        bu = b_gu[l][:, 1::2].reshape(n_exp, 1, -1)
        yp, ysm = _moe(joint, n_p, row(norm_moe[l]), w_router[l].T, b_router[l].reshape(n_exp, 1), w_gu[l], bg, bu,
                       w_down[l], b_down[l].reshape(n_exp, 1, d), row(norm_final), l == depth - 1)
        xp, xs = yp, ysm

    st = lambda xs_: jnp.stack(xs_)
    return (xp.reshape(bp, sp, d), xs.reshape(bs, ts, d), st(k_p), st(v_p), st(cs_p), st(mk_p), st(mv_p), st(k_s),
            st(v_s), st(cs_s))
```
